```python
import jax, jax.numpy as jnp
from jax import lax
import numpy as np

D_MODEL = 2048
BATCH = 2
SEQ = 4096
DEPTH = 4

LRU_WIDTH = D_MODEL // 2
LRU_HEADS = 8
LRU_HEAD_DIM = LRU_WIDTH // LRU_HEADS
LRU_CONV = 4
LRU_C = 8.0
HGRN_HEADS = 8
HGRN_DK = (D_MODEL // 2) // HGRN_HEADS
HGRN_DV = (D_MODEL // 2) // HGRN_HEADS
HGRN_WIDTH = HGRN_HEADS * HGRN_DK
HGRN_CHUNK = 64
SC_WIDTH = D_MODEL // 2
SC_CONV = 3
CF_WIDTH = D_MODEL // 2
CF_CONV = 31
FFN_HIDDEN = -(-8 * D_MODEL // (3 * 256)) * 256
N_EVEN = (DEPTH + 1) // 2
N_ODD = DEPTH // 2
EV_IN = 2 * LRU_WIDTH + 4 * HGRN_WIDTH
EV_OUT = LRU_WIDTH + HGRN_HEADS * HGRN_DV
OD_IN = 3 * SC_WIDTH + 2 * CF_WIDTH
OD_OUT = SC_WIDTH + CF_WIDTH
EPS = 1e-6
F_FLOOR = 1e-30

kernel_name = "hybrid_rglru_hgrn2_shortconv_conformer_trunk"


def rmsnorm(x, g):
    xf = x.astype(jnp.float32)
    y = xf * lax.rsqrt(jnp.mean(xf * xf, axis=-1, keepdims=True) + EPS)
    return (y * g.astype(jnp.float32)).astype(x.dtype)


def layernorm(x, g, b):
    xf = x.astype(jnp.float32)
    mu = jnp.mean(xf, axis=-1, keepdims=True)
    xc = xf - mu
    y = xc * lax.rsqrt(jnp.mean(xc * xc, axis=-1, keepdims=True) + EPS)
    return (y * g.astype(jnp.float32) + b.astype(jnp.float32)).astype(x.dtype)


def split_cols(t, sizes):
    outs, off = [], 0
    for s in sizes:
        outs.append(t[..., off:off + s])
        off += s
    return outs


def causal_dwconv(x, w, b=None):
    K, C = w.shape
    y = lax.conv_general_dilated(
        x, w[:, None, :].astype(x.dtype), window_strides=(1,), padding=[(K - 1, 0)],
        dimension_numbers=("NWC", "WIO", "NWC"), feature_group_count=C)
    if b is not None:
        y = y + b.astype(x.dtype)
    return y


def rg_lru(x, wa, ba, wx, bx, lam):
    Bsz, S, W = x.shape
    xh = x.reshape(Bsz, S, LRU_HEADS, LRU_HEAD_DIM)
    r = jax.nn.sigmoid((jnp.einsum('bshd,hde->bshe', xh, wa) + ba).astype(jnp.float32)).reshape(Bsz, S, W)
    i = jax.nn.sigmoid((jnp.einsum('bshd,hde->bshe', xh, wx) + bx).astype(jnp.float32)).reshape(Bsz, S, W)
    log_a = -LRU_C * r * jax.nn.softplus(-lam.astype(jnp.float32))
    a = jnp.exp(log_a)
    mult = jnp.sqrt(jnp.maximum(-jnp.expm1(2.0 * log_a), 0.0))
    mult = mult.at[:, 0].set(1.0)
    u = mult * i * x.astype(jnp.float32)

    def combine(left, right):
        a_l, b_l = left
        a_r, b_r = right
        return a_l * a_r, a_r * b_l + b_r

    _, h = lax.associative_scan(combine, (a, u), axis=1)
    return h


def hgrn2_chunked(q, k, v, log_f):
    Bsz, S, H, DK = q.shape
    DV = v.shape[-1]
    C = HGRN_CHUNK
    NC = S // C

    def to_chunks(t):
        return t.reshape(Bsz, NC, C, H, t.shape[-1]).transpose(1, 0, 3, 2, 4)

    qc, kc, vc, gc = to_chunks(q), to_chunks(k), to_chunks(v), to_chunks(log_f)
    bc = jnp.cumsum(gc, axis=3)
    causal = jnp.tril(jnp.ones((C, C), dtype=bool))[:, :, None]

    def step(state, inp):
        q_, k_, v_, b_ = inp
        inter = jnp.einsum('bhtk,bhkv->bhtv', q_ * jnp.exp(b_), state)
        diff = b_[:, :, :, None, :] - b_[:, :, None, :, :]
        decay = jnp.where(causal, jnp.exp(jnp.where(causal, diff, 0.0)), 0.0)
        scores = jnp.einsum('bhtk,bhtsk,bhsk->bhts', q_, decay, k_)
        intra = jnp.einsum('bhts,bhsv->bhtv', scores, v_)
        b_last = b_[:, :, -1:, :]
        new_state = (jnp.exp(b_last[:, :, 0, :])[..., None] * state
                     + jnp.einsum('bhsk,bhsv->bhkv', k_ * jnp.exp(b_last - b_), v_))
        return new_state, inter + intra

    state0 = jnp.zeros((Bsz, H, DK, DV), jnp.float32)
    _, o = lax.scan(step, state0, (qc, kc, vc, bc))
    return o.transpose(1, 0, 3, 2, 4).reshape(Bsz, S, H, DV)


def even_mixer(h, w_in, b_in, conv_w, conv_b, wa, ba, wx, bx, lam, lb, norm_g, w_out):
    Bsz, S, _ = h.shape
    proj = h @ w_in + b_in
    x_a, gate_a, q, f, i, g = split_cols(
        proj, [LRU_WIDTH, LRU_WIDTH, HGRN_WIDTH, HGRN_WIDTH, HGRN_HEADS * HGRN_DV, HGRN_HEADS * HGRN_DV])
    x_a = causal_dwconv(x_a, conv_w, conv_b)
    h_a = rg_lru(x_a, wa, ba, wx, bx, lam)
    y_a = (h_a * jax.nn.gelu(gate_a.astype(jnp.float32))).astype(h.dtype)
    z = f.astype(jnp.float32).reshape(Bsz, S, HGRN_HEADS, HGRN_DK)
    lb = lb.astype(jnp.float32).reshape(HGRN_HEADS, HGRN_DK)
    sig = jax.nn.sigmoid(z)
    f_gate = lb + (1.0 - lb) * sig
    log_f = jnp.log(jnp.maximum(f_gate, F_FLOOR))
    k = (1.0 - lb) * (1.0 - sig)
    qf = jax.nn.silu(q.astype(jnp.float32)).reshape(Bsz, S, HGRN_HEADS, HGRN_DK)
    vf = i.astype(jnp.float32).reshape(Bsz, S, HGRN_HEADS, HGRN_DV)
    o = hgrn2_chunked(qf, k, vf, log_f)
    o = o * lax.rsqrt(jnp.mean(o * o, axis=-1, keepdims=True) + EPS)
    o = o.reshape(Bsz, S, HGRN_HEADS * HGRN_DV) * norm_g.astype(jnp.float32)
    y_b = (o * jax.nn.silu(g.astype(jnp.float32))).astype(h.dtype)
    return jnp.concatenate([y_a, y_b], axis=-1) @ w_out


def odd_mixer(h, w_in, b_in, sc_w, cf_w, cf_b, cf_g, cf_beta, w_out):
    proj = h @ w_in + b_in
    sb, sc, sv, cu, cg = split_cols(proj, [SC_WIDTH, SC_WIDTH, SC_WIDTH, CF_WIDTH, CF_WIDTH])
    y_c = sb * causal_dwconv(sc * sv, sc_w)
    glu = cu * jax.nn.sigmoid(cg)
    d = causal_dwconv(glu, cf_w, cf_b)
    y_d = jax.nn.silu(layernorm(d, cf_g, cf_beta))
    return jnp.concatenate([y_c, y_d], axis=-1) @ w_out


def swiglu(h, wg, wu, wd):
    return (jax.nn.silu(h @ wg) * (h @ wu)) @ wd


def setup_inputs(seed: int = 0) -> dict:
    key = jax.random.key(seed)
    ks = iter(jax.random.split(key, 40))
    f32 = jnp.float32

    def nrm(shape, scale):
        return jax.random.normal(next(ks), shape, f32) * scale

    def gain(shape):
        return 1.0 + 0.02 * jax.random.normal(next(ks), shape, f32)

    u = jax.random.uniform(next(ks), (N_EVEN, LRU_WIDTH), f32, 0.9, 0.999)
    a_base = u ** (1.0 / LRU_C)
    lru_lambda = jnp.log(a_base) - jnp.log1p(-a_base)
    return {
        "x": jax.random.normal(next(ks), (BATCH, SEQ, D_MODEL), f32),
        "ln_mix_g": gain((DEPTH, D_MODEL)),
        "ln_ffn_g": gain((DEPTH, D_MODEL)),
        "ln_final_g": gain((D_MODEL,)),
        "ev_w_in": nrm((N_EVEN, D_MODEL, EV_IN), D_MODEL ** -0.5),
        "ev_b_in": nrm((N_EVEN, EV_IN), 0.01),
        "lru_conv_w": nrm((N_EVEN, LRU_CONV, LRU_WIDTH), LRU_CONV ** -0.5),
        "lru_conv_b": nrm((N_EVEN, LRU_WIDTH), 0.01),
        "lru_wa": nrm((N_EVEN, LRU_HEADS, LRU_HEAD_DIM, LRU_HEAD_DIM), LRU_HEAD_DIM ** -0.5),
        "lru_ba": nrm((N_EVEN, LRU_HEADS, LRU_HEAD_DIM), 0.01),
        "lru_wx": nrm((N_EVEN, LRU_HEADS, LRU_HEAD_DIM, LRU_HEAD_DIM), LRU_HEAD_DIM ** -0.5),
        "lru_bx": nrm((N_EVEN, LRU_HEADS, LRU_HEAD_DIM), 0.01),
        "lru_lambda": lru_lambda,
        "hgrn_lb_logits": nrm((N_EVEN, HGRN_WIDTH), 0.1),
        "hgrn_norm_g": gain((N_EVEN, HGRN_HEADS * HGRN_DV)),
        "ev_w_out": nrm((N_EVEN, EV_OUT, D_MODEL), EV_OUT ** -0.5),
        "od_w_in": nrm((N_ODD, D_MODEL, OD_IN), D_MODEL ** -0.5),
        "od_b_in": nrm((N_ODD, OD_IN), 0.01),
        "sc_conv_w": nrm((N_ODD, SC_CONV, SC_WIDTH), SC_CONV ** -0.5),
        "cf_conv_w": nrm((N_ODD, CF_CONV, CF_WIDTH), CF_CONV ** -0.5),
        "cf_conv_b": nrm((N_ODD, CF_WIDTH), 0.01),
        "cf_ln_g": gain((N_ODD, CF_WIDTH)),
        "cf_ln_b": nrm((N_ODD, CF_WIDTH), 0.01),
        "od_w_out": nrm((N_ODD, OD_OUT, D_MODEL), OD_OUT ** -0.5),
        "ffn_w_gate": nrm((DEPTH, D_MODEL, FFN_HIDDEN), D_MODEL ** -0.5),
        "ffn_w_up": nrm((DEPTH, D_MODEL, FFN_HIDDEN), D_MODEL ** -0.5),
        "ffn_w_down": nrm((DEPTH, FFN_HIDDEN, D_MODEL), FFN_HIDDEN ** -0.5),
    }


def reference(x, ln_mix_g, ln_ffn_g, ln_final_g, ev_w_in, ev_b_in, lru_conv_w, lru_conv_b,
              lru_wa, lru_ba, lru_wx, lru_bx, lru_lambda, hgrn_lb_logits, hgrn_norm_g, ev_w_out,
              od_w_in, od_b_in, sc_conv_w, cf_conv_w, cf_conv_b, cf_ln_g, cf_ln_b, od_w_out,
              ffn_w_gate, ffn_w_up, ffn_w_down):
    sm = jax.nn.softmax(hgrn_lb_logits.astype(jnp.float32), axis=0)
    lower_bounds = jnp.cumsum(sm, axis=0) - sm[0]
    for layer in range(DEPTH):
        h = rmsnorm(x, ln_mix_g[layer])
        if layer % 2 == 0:
            j = layer // 2
            x = x + even_mixer(h, ev_w_in[j], ev_b_in[j], lru_conv_w[j], lru_conv_b[j],
                               lru_wa[j], lru_ba[j], lru_wx[j], lru_bx[j], lru_lambda[j],
                               lower_bounds[j], hgrn_norm_g[j], ev_w_out[j])
        else:
            j = layer // 2
            x = x + odd_mixer(h, od_w_in[j], od_b_in[j], sc_conv_w[j], cf_conv_w[j], cf_conv_b[j],
                              cf_ln_g[j], cf_ln_b[j], od_w_out[j])
        h = rmsnorm(x, ln_ffn_g[layer])
        x = x + swiglu(h, ffn_w_gate[layer], ffn_w_up[layer], ffn_w_down[layer])
    return rmsnorm(x, ln_final_g)
```

```python
import functools

import jax
import jax.numpy as jnp
from jax import lax
from jax.experimental import pallas as pl
from jax.experimental.pallas import tpu as pltpu

F32 = jnp.float32
BF16 = jnp.bfloat16

EPS = 1e-6
F_FLOOR = 1e-30
LRU_C = 8.0
LRU_HEADS = 8
LRU_CONV = 4
HGRN_HEADS = 8
HEAD_DIM = 128
SC_CONV = 3
CF_CONV = 31
HGRN_SUB = 16

V7X_LANES = 128
V7X_SUBLANES = 8
V7X_VMEM_BYTES = 64 * 2**20
MIB = 2**20


def _params(semantics, vmem_bytes):
    assert vmem_bytes <= V7X_VMEM_BYTES
    return pltpu.CompilerParams(dimension_semantics=semantics, vmem_limit_bytes=int(vmem_bytes))


def _sigmoid(x):
    return jax.nn.sigmoid(x)


def _rmsnorm_body(x_ref, g_ref, o_ref):
    x = x_ref[...]
    ms = jnp.mean(x * x, axis=-1, keepdims=True)
    o_ref[...] = ((x * lax.rsqrt(ms + EPS)) * g_ref[...]).astype(o_ref.dtype)


def rmsnorm(x, g_stack, layer, out_dtype, tm=512):
    T, D = x.shape
    out_bytes = jnp.dtype(out_dtype).itemsize
    vmem = 2 * tm * D * (4 + out_bytes) + 8 * MIB
    return pl.pallas_call(
        _rmsnorm_body,
        grid=(T // tm,),
        in_specs=[
            pl.BlockSpec((tm, D), lambda i: (i, 0)),
            pl.BlockSpec((None, 1, D), lambda i: (layer, 0, 0)),
        ],
        out_specs=pl.BlockSpec((tm, D), lambda i: (i, 0)),
        out_shape=jax.ShapeDtypeStruct((T, D), out_dtype),
        compiler_params=_params(("arbitrary",), vmem),
        name="rmsnorm",
    )(x, g_stack)


def _proj_body(h_ref, w_ref, b_ref, o_ref, wbf_ref):
    @pl.when(pl.program_id(1) == 0)
    def _():
        wbf_ref[...] = w_ref[...].astype(BF16)

    acc = jnp.dot(h_ref[...], wbf_ref[...], preferred_element_type=F32)
    o_ref[...] = acc + b_ref[...]


def proj(h, w_stack, b_stack, layer, tm=512, tn=512):
    T, K = h.shape
    N = w_stack.shape[-1]
    vmem = 2 * K * tn * 4 + K * tn * 2 + 2 * tm * K * 2 + 2 * tm * tn * 4 + 8 * MIB
    return pl.pallas_call(
        _proj_body,
        grid=(N // tn, T // tm),
        in_specs=[
            pl.BlockSpec((tm, K), lambda n, m: (m, 0)),
            pl.BlockSpec((None, K, tn), lambda n, m: (layer, 0, n)),
            pl.BlockSpec((None, 1, tn), lambda n, m: (layer, 0, n)),
        ],
        out_specs=pl.BlockSpec((tm, tn), lambda n, m: (m, n)),
        out_shape=jax.ShapeDtypeStruct((T, N), F32),
        scratch_shapes=[pltpu.VMEM((K, tn), BF16)],
        compiler_params=_params(("arbitrary", "arbitrary"), vmem),
        name="in_proj",
    )(h, w_stack, b_stack)


def _out_body(ya_ref, yb_ref, wa_ref, wb_ref, x_ref, o_ref, wa_bf, wb_bf):
    @pl.when(pl.program_id(1) == 0)
    def _():
        wa_bf[...] = wa_ref[...].astype(BF16)
        wb_bf[...] = wb_ref[...].astype(BF16)

    acc = jnp.dot(ya_ref[...], wa_bf[...], preferred_element_type=F32)
    acc = acc + jnp.dot(yb_ref[...], wb_bf[...], preferred_element_type=F32)
    o_ref[...] = x_ref[...] + acc


def out_proj(ya, yb, w_stack, layer, x, tm=512, tn=512):
    T, Kh = ya.shape
    N = w_stack.shape[-1]
    vmem = 2 * (2 * Kh * tn * 4) + 2 * Kh * tn * 2 + 2 * 2 * tm * Kh * 2 + 4 * tm * tn * 4 + 8 * MIB
    return pl.pallas_call(
        _out_body,
        grid=(N // tn, T // tm),
        in_specs=[
            pl.BlockSpec((tm, Kh), lambda n, m: (m, 0)),
            pl.BlockSpec((tm, Kh), lambda n, m: (m, 0)),
            pl.BlockSpec((None, Kh, tn), lambda n, m: (layer, 0, n)),
            pl.BlockSpec((None, Kh, tn), lambda n, m: (layer, 1, n)),
            pl.BlockSpec((tm, tn), lambda n, m: (m, n)),
        ],
        out_specs=pl.BlockSpec((tm, tn), lambda n, m: (m, n)),
        out_shape=jax.ShapeDtypeStruct((T, N), F32),
        scratch_shapes=[pltpu.VMEM((Kh, tn), BF16), pltpu.VMEM((Kh, tn), BF16)],
        compiler_params=_params(("arbitrary", "arbitrary"), vmem),
        name="out_proj",
    )(ya, yb, w_stack, w_stack, x)


def _ffn_up_body(h_ref, wg_ref, wu_ref, o_ref, wg_bf, wu_bf):
    @pl.when(pl.program_id(1) == 0)
    def _():
        wg_bf[...] = wg_ref[...].astype(BF16)
        wu_bf[...] = wu_ref[...].astype(BF16)

    h = h_ref[...]
    g = jnp.dot(h, wg_bf[...], preferred_element_type=F32)
    u = jnp.dot(h, wu_bf[...], preferred_element_type=F32)
    o_ref[...] = ((g * _sigmoid(g)) * u).astype(o_ref.dtype)


def ffn_up(h, wg_stack, wu_stack, layer, tm=512, tn=512):
    T, K = h.shape
    F = wg_stack.shape[-1]
    vmem = 2 * 2 * K * tn * 4 + 2 * K * tn * 2 + 2 * tm * K * 2 + 2 * tm * tn * 2 + 12 * MIB
    return pl.pallas_call(
        _ffn_up_body,
        grid=(F // tn, T // tm),
        in_specs=[
            pl.BlockSpec((tm, K), lambda n, m: (m, 0)),
            pl.BlockSpec((None, K, tn), lambda n, m: (layer, 0, n)),
            pl.BlockSpec((None, K, tn), lambda n, m: (layer, 0, n)),
        ],
        out_specs=pl.BlockSpec((tm, tn), lambda n, m: (m, n)),
        out_shape=jax.ShapeDtypeStruct((T, F), BF16),
        scratch_shapes=[pltpu.VMEM((K, tn), BF16), pltpu.VMEM((K, tn), BF16)],
        compiler_params=_params(("arbitrary", "arbitrary"), vmem),
        name="ffn_up",
    )(h, wg_stack, wu_stack)


def _ffn_down_body(a_ref, w_ref, x_ref, o_ref, wbf_ref):
    @pl.when(pl.program_id(1) == 0)
    def _():
        wbf_ref[...] = w_ref[...].astype(BF16)

    acc = jnp.dot(a_ref[...], wbf_ref[...], preferred_element_type=F32)
    o_ref[...] = x_ref[...] + acc


def ffn_down(a, w_stack, layer, x, tm=512, tn=256):
    T, F = a.shape
    N = w_stack.shape[-1]
    vmem = 2 * F * tn * 4 + F * tn * 2 + 2 * tm * F * 2 + 4 * tm * tn * 4 + 8 * MIB
    return pl.pallas_call(
        _ffn_down_body,
        grid=(N // tn, T // tm),
        in_specs=[
            pl.BlockSpec((tm, F), lambda n, m: (m, 0)),
            pl.BlockSpec((None, F, tn), lambda n, m: (layer, 0, n)),
            pl.BlockSpec((tm, tn), lambda n, m: (m, n)),
        ],
        out_specs=pl.BlockSpec((tm, tn), lambda n, m: (m, n)),
        out_shape=jax.ShapeDtypeStruct((T, N), F32),
        scratch_shapes=[pltpu.VMEM((F, tn), BF16)],
        compiler_params=_params(("arbitrary", "arbitrary"), vmem),
        name="ffn_down",
    )(a, w_stack, x)


def _shift_rows(x, d, fill):
    top = jnp.full((d, x.shape[1]), fill, x.dtype)
    return jnp.concatenate([top, x[: x.shape[0] - d]], axis=0)


def _lru_body(xa_ref, gate_ref, cw_ref, cb_ref, wg_ref, ba_ref, bx_ref, lam_ref, y_ref,
              xe_ref, hc_ref, a_sc, u_sc, *, ts):
    i = pl.program_id(1)
    pad = V7X_SUBLANES
    width = xa_ref.shape[1]

    @pl.when(i == 0)
    def _():
        xe_ref[0:pad, :] = jnp.zeros((pad, width), F32)
        hc_ref[...] = jnp.zeros(hc_ref.shape, F32)
        a_sc[0:pad, :] = jnp.ones((pad, width), F32)
        u_sc[0:pad, :] = jnp.zeros((pad, width), F32)

    xe_ref[pad:pad + ts, :] = xa_ref[...]
    xc = cw_ref[0:1, :] * xe_ref[pl.ds(pad - (LRU_CONV - 1), ts), :]
    for k in range(1, LRU_CONV):
        xc = xc + cw_ref[k:k + 1, :] * xe_ref[pl.ds(pad - (LRU_CONV - 1) + k, ts), :]
    xc = xc + cb_ref[...]
    xe_ref[0:pad, :] = xe_ref[ts:ts + pad, :]

    xcb = xc.astype(BF16)
    r_parts, i_parts = [], []
    for h in range(LRU_HEADS):
        gh = jnp.dot(xcb[:, h * HEAD_DIM:(h + 1) * HEAD_DIM], wg_ref[h].astype(BF16),
                     preferred_element_type=F32)
        r_parts.append(gh[:, :HEAD_DIM])
        i_parts.append(gh[:, HEAD_DIM:])
    r = _sigmoid(jnp.concatenate(r_parts, axis=1) + ba_ref[...])
    ig = _sigmoid(jnp.concatenate(i_parts, axis=1) + bx_ref[...])

    nl = -lam_ref[...]
    softplus = jnp.maximum(nl, 0.0) + jnp.log1p(jnp.exp(-jnp.abs(nl)))
    log_a = (-LRU_C * r) * softplus
    a = jnp.exp(log_a)
    mult = jnp.sqrt(jnp.maximum(jnp.tanh(-log_a) * (a * a + 1.0), 0.0))
    row = lax.broadcasted_iota(jnp.int32, (ts, width), 0)
    mult = jnp.where(jnp.logical_and(i == 0, row == 0), 1.0, mult)
    u = (mult * ig) * xc

    d = 1
    while d < pad:
        a_sc[pad:pad + ts, :] = a
        u_sc[pad:pad + ts, :] = u
        a_s = a_sc[pl.ds(pad - d, ts), :]
        u_s = u_sc[pl.ds(pad - d, ts), :]
        u = a * u_s + u
        a = a * a_s
        d *= 2
    while d < ts:
        u = a * _shift_rows(u, d, 0.0) + u
        a = a * _shift_rows(a, d, 1.0)
        d *= 2
    hseq = a * hc_ref[0:1, :] + u
    hc_ref[0:1, :] = hseq[ts - 1:ts, :]

    g = gate_ref[...]
    c = 0.7978845608028654
    cdf = 0.5 * (1.0 + jnp.tanh(c * (g + 0.044715 * (g * g * g))))
    y_ref[...] = (hseq * (g * cdf)).astype(y_ref.dtype)


def lru_mixer(proj_out, batch, conv_w, conv_b, w_gates, ba, bx, lam, layer, ts=256):
    T = proj_out.shape[0]
    W = LRU_HEADS * HEAD_DIM
    nt = T // batch // ts
    vmem = 2 * 2 * ts * W * 4 + 2 * ts * W * 2 + 3 * (ts + 8) * W * 4 + 24 * ts * W * 4 + 8 * MIB
    vec = pl.BlockSpec((None, 1, W), lambda b, i: (layer, 0, 0))
    return pl.pallas_call(
        functools.partial(_lru_body, ts=ts),
        grid=(batch, nt),
        in_specs=[
            pl.BlockSpec((ts, W), lambda b, i: (b * nt + i, 0)),
            pl.BlockSpec((ts, W), lambda b, i: (b * nt + i, 1)),
            pl.BlockSpec((None, LRU_CONV, W), lambda b, i: (layer, 0, 0)),
            vec,
            pl.BlockSpec((None, LRU_HEADS, HEAD_DIM, 2 * HEAD_DIM), lambda b, i: (layer, 0, 0, 0)),
            vec, vec, vec,
        ],
        out_specs=pl.BlockSpec((ts, W), lambda b, i: (b * nt + i, 0)),
        out_shape=jax.ShapeDtypeStruct((T, W), BF16),
        scratch_shapes=[
            pltpu.VMEM((ts + 8, W), F32),
            pltpu.VMEM((8, W), F32),
            pltpu.VMEM((ts + 8, W), F32),
            pltpu.VMEM((ts + 8, W), F32),
        ],
        compiler_params=_params(("arbitrary", "arbitrary"), min(vmem, 56 * MIB)),
        name="rg_lru",
    )(proj_out, proj_out, conv_w, conv_b, w_gates, ba, bx, lam)


def _cumsum_rows(x):
    n = x.shape[0]
    row = lax.broadcasted_iota(jnp.int32, x.shape, 0)
    d = 1
    while d < n:
        x = x + jnp.where(row >= d, pltpu.roll(x, d, 0), 0.0)
        d *= 2
    return x


def _hgrn_body(q_ref, f_ref, v_ref, g_ref, lbl_ref, ng_ref, y_ref, st_ref, *, ts, layer):
    t_idx = pl.program_id(2)

    @pl.when(t_idx == 0)
    def _():
        st_ref[...] = jnp.zeros(st_ref.shape, F32)

    logits = lbl_ref[...]
    e = jnp.exp(logits - jnp.max(logits, axis=0, keepdims=True))
    sm = e / jnp.sum(e, axis=0, keepdims=True)
    lb = jnp.sum(sm[0:layer + 1, :], axis=0, keepdims=True) - sm[0:1, :]
    ng = ng_ref[...]
    n = HGRN_SUB
    row = lax.broadcasted_iota(jnp.int32, (n, HEAD_DIM), 0)

    def sub_block(j, carry):
        r0 = pl.multiple_of(j * n, n)
        f = f_ref[pl.ds(r0, n), :]
        q = q_ref[pl.ds(r0, n), :]
        v = v_ref[pl.ds(r0, n), :]
        sig = _sigmoid(f)
        log_f = jnp.log(jnp.maximum(lb + (1.0 - lb) * sig, F_FLOOR))
        k = (1.0 - lb) * (1.0 - sig)
        qf = q * _sigmoid(q)
        bc = _cumsum_rows(log_f)
        b_tot = bc[n - 1:n, :]

        st = st_ref[...]
        qd = (qf * jnp.exp(bc)).astype(BF16)
        o = lax.dot_general(qd, st.astype(BF16), (((1,), (1,)), ((), ())),
                            preferred_element_type=F32)
        for s in range(n):
            dec = jnp.where(row >= s, jnp.exp(bc - bc[s:s + 1, :]), 0.0)
            w = jnp.sum((qf * dec) * k[s:s + 1, :], axis=-1, keepdims=True)
            o = o + w * v[s:s + 1, :]
        kd = (k * jnp.exp(b_tot - bc)).astype(BF16)
        d_st = lax.dot_general(v.astype(BF16), kd, (((0,), (0,)), ((), ())),
                               preferred_element_type=F32)
        st_ref[...] = st * jnp.exp(b_tot) + d_st

        o = o * lax.rsqrt(jnp.mean(o * o, axis=-1, keepdims=True) + EPS)
        o = o * ng
        g = g_ref[pl.ds(r0, n), :]
        y_ref[pl.ds(r0, n), :] = (o * (g * _sigmoid(g))).astype(y_ref.dtype)
        return carry

    lax.fori_loop(0, ts // n, sub_block, 0)


def hgrn_mixer(proj_out, batch, lb_logits, norm_g, layer, ts=256):
    T = proj_out.shape[0]
    W = HGRN_HEADS * HEAD_DIM
    H = HGRN_HEADS
    nt = T // batch // ts
    n_layers = lb_logits.shape[0]

    def col(section):
        return pl.BlockSpec((ts, HEAD_DIM), lambda b, h, i: (b * nt + i, section * H + h))

    return pl.pallas_call(
        functools.partial(_hgrn_body, ts=ts, layer=layer),
        grid=(batch, H, nt),
        in_specs=[
            col(2), col(3), col(4), col(5),
            pl.BlockSpec((n_layers, HEAD_DIM), lambda b, h, i: (0, h)),
            pl.BlockSpec((None, 1, HEAD_DIM), lambda b, h, i: (layer, 0, h)),
        ],
        out_specs=pl.BlockSpec((ts, HEAD_DIM), lambda b, h, i: (b * nt + i, h)),
        out_shape=jax.ShapeDtypeStruct((T, W), BF16),
        scratch_shapes=[pltpu.VMEM((HEAD_DIM, HEAD_DIM), F32)],
        compiler_params=_params(("arbitrary", "arbitrary", "arbitrary"), 16 * MIB),
        name="hgrn2",
    )(proj_out, proj_out, proj_out, proj_out, lb_logits, norm_g)


def _odd_body(sb_ref, sc_ref, sv_ref, cu_ref, cg_ref, scw_ref, cfw_ref, cfb_ref, lng_ref, lnb_ref,
              yc_ref, yd_ref, pe_ref, ge_ref, *, ts, rows):
    i = pl.program_id(1)
    width = sb_ref.shape[1]
    pad_p = V7X_SUBLANES
    pad_g = 4 * V7X_SUBLANES

    @pl.when(i == 0)
    def _():
        pe_ref[0:pad_p, :] = jnp.zeros((pad_p, width), F32)
        ge_ref[0:pad_g, :] = jnp.zeros((pad_g, width), F32)

    pe_ref[pad_p:pad_p + ts, :] = sc_ref[...] * sv_ref[...]
    cu = cu_ref[...]
    ge_ref[pad_g:pad_g + ts, :] = cu * _sigmoid(cg_ref[...])

    for c in range(ts // rows):
        r0 = c * rows
        base = pad_p - (SC_CONV - 1) + r0
        acc = scw_ref[0:1, :] * pe_ref[base:base + rows, :]
        for k in range(1, SC_CONV):
            acc = acc + scw_ref[k:k + 1, :] * pe_ref[base + k:base + k + rows, :]
        yc_ref[r0:r0 + rows, :] = (sb_ref[r0:r0 + rows, :] * acc).astype(yc_ref.dtype)

        base = pad_g - (CF_CONV - 1) + r0
        acc = cfw_ref[0:1, :] * ge_ref[base:base + rows, :]
        for k in range(1, CF_CONV):
            acc = acc + cfw_ref[k:k + 1, :] * ge_ref[base + k:base + k + rows, :]
        d = acc + cfb_ref[...]
        mu = jnp.mean(d, axis=-1, keepdims=True)
        dc = d - mu
        var = jnp.mean(dc * dc, axis=-1, keepdims=True)
        z = (dc * lax.rsqrt(var + EPS)) * lng_ref[...] + lnb_ref[...]
        yd_ref[r0:r0 + rows, :] = (z * _sigmoid(z)).astype(yd_ref.dtype)

    pe_ref[0:pad_p, :] = pe_ref[ts:ts + pad_p, :]
    ge_ref[0:pad_g, :] = ge_ref[ts:ts + pad_g, :]


def odd_mixer(proj_out, batch, sc_w, cf_w, cf_b, ln_g, ln_b, layer, ts=256, rows=32):
    T = proj_out.shape[0]
    W = proj_out.shape[1] // 5
    nt = T // batch // ts
    vmem = 2 * 5 * ts * W * 4 + 2 * 2 * ts * W * 2 + 2 * (ts + 32) * W * 4 + 32 * MIB

    def col(section):
        return pl.BlockSpec((ts, W), lambda b, i: (b * nt + i, section))

    vec = pl.BlockSpec((None, 1, W), lambda b, i: (layer, 0, 0))
    out = pl.BlockSpec((ts, W), lambda b, i: (b * nt + i, 0))
    return pl.pallas_call(
        functools.partial(_odd_body, ts=ts, rows=rows),
        grid=(batch, nt),
        in_specs=[
            col(0), col(1), col(2), col(3), col(4),
            pl.BlockSpec((None, SC_CONV, W), lambda b, i: (layer, 0, 0)),
            pl.BlockSpec((None, CF_CONV, W), lambda b, i: (layer, 0, 0)),
            vec, vec, vec,
        ],
        out_specs=[out, out],
        out_shape=[jax.ShapeDtypeStruct((T, W), BF16), jax.ShapeDtypeStruct((T, W), BF16)],
        scratch_shapes=[
            pltpu.VMEM((ts + 8, W), F32),
            pltpu.VMEM((ts + 32, W), F32),
        ],
        compiler_params=_params(("arbitrary", "arbitrary"), vmem),
        name="odd_mixer",
    )(proj_out, proj_out, proj_out, proj_out, proj_out, sc_w, cf_w, cf_b, ln_g, ln_b)


def kernel(x, ln_mix_g, ln_ffn_g, ln_final_g, ev_w_in, ev_b_in, lru_conv_w, lru_conv_b, lru_wa, lru_ba, lru_wx, lru_bx, lru_lambda, hgrn_lb_logits, hgrn_norm_g, ev_w_out, od_w_in, od_b_in, sc_conv_w, cf_conv_w, cf_conv_b, cf_ln_g, cf_ln_b, od_w_out, ffn_w_gate, ffn_w_up, ffn_w_down):
    B, S, D = x.shape
    depth = ln_mix_g.shape[0]
    xt = x.reshape(B * S, D)

    def rows(p):
        return p.reshape(p.shape[0], 1, p.shape[-1])

    ln_mix = rows(ln_mix_g)
    ln_ffn = rows(ln_ffn_g)
    ln_fin = ln_final_g.reshape(1, 1, D)
    ev_b = rows(ev_b_in)
    od_b = rows(od_b_in)
    w_gates = jnp.concatenate([lru_wa, lru_wx], axis=-1)
    n_even = lru_wa.shape[0]
    lru_ba2 = lru_ba.reshape(n_even, 1, -1)
    lru_bx2 = lru_bx.reshape(n_even, 1, -1)

    for layer in range(depth):
        j = layer // 2
        h = rmsnorm(xt, ln_mix, layer, BF16)
        if layer % 2 == 0:
            p = proj(h, ev_w_in, ev_b, j)
            ya = lru_mixer(p, B, lru_conv_w, rows(lru_conv_b), w_gates, lru_ba2, lru_bx2,
                           rows(lru_lambda), j)
            yb = hgrn_mixer(p, B, hgrn_lb_logits, rows(hgrn_norm_g), j)
            xt = out_proj(ya, yb, ev_w_out, j, xt)
        else:
            p = proj(h, od_w_in, od_b, j)
            yc, yd = odd_mixer(p, B, sc_conv_w, cf_conv_w, rows(cf_conv_b), rows(cf_ln_g),
                               rows(cf_ln_b), j)
            xt = out_proj(yc, yd, od_w_out, j, xt)
        h = rmsnorm(xt, ln_ffn, layer, BF16)
        a = ffn_up(h, ffn_w_gate, ffn_w_up, layer)
        xt = ffn_down(a, ffn_w_down, layer, xt)
    out = rmsnorm(xt, ln_fin, 0, F32)
    return out.reshape(B, S, D)
```

```python
import functools

import jax
import jax.numpy as jnp
from jax import lax
from jax.experimental import pallas as pl
from jax.experimental.pallas import tpu as pltpu

F32 = jnp.float32
BF16 = jnp.bfloat16

EPS = 1e-6
F_FLOOR = 1e-30
LRU_C = 8.0
LRU_HEADS = 8
LRU_CONV = 4
HGRN_HEADS = 8
HEAD_DIM = 128
SC_CONV = 3
CF_CONV = 31
HGRN_SUB = 16
LOG2_E = 1.4426950408889634

V7X_LANES = 128
V7X_SUBLANES = 8
V7X_VMEM_BYTES = 64 * 2**20
MIB = 2**20


def _params(semantics, vmem_bytes):
    assert vmem_bytes <= V7X_VMEM_BYTES
    return pltpu.CompilerParams(dimension_semantics=semantics, vmem_limit_bytes=int(vmem_bytes))


def _sigmoid(x):
    return jax.nn.sigmoid(x)


def _rmsnorm_body(x_ref, g_ref, o_ref):
    x = x_ref[...]
    ms = jnp.mean(x * x, axis=-1, keepdims=True)
    o_ref[...] = ((x * lax.rsqrt(ms + EPS)) * g_ref[...]).astype(o_ref.dtype)


def rmsnorm(x, g_stack, layer, out_dtype, tm=512):
    T, D = x.shape
    out_bytes = jnp.dtype(out_dtype).itemsize
    vmem = 2 * tm * D * (4 + out_bytes) + 8 * MIB
    return pl.pallas_call(
        _rmsnorm_body,
        grid=(T // tm,),
        in_specs=[
            pl.BlockSpec((tm, D), lambda i: (i, 0)),
            pl.BlockSpec((None, 1, D), lambda i: (layer, 0, 0)),
        ],
        out_specs=pl.BlockSpec((tm, D), lambda i: (i, 0)),
        out_shape=jax.ShapeDtypeStruct((T, D), out_dtype),
        compiler_params=_params(("arbitrary",), vmem),
        name="rmsnorm",
    )(x, g_stack)


def _proj_body(h_ref, w_ref, b_ref, o_ref, wbf_ref):
    @pl.when(pl.program_id(1) == 0)
    def _():
        wbf_ref[...] = w_ref[...].astype(BF16)

    acc = jnp.dot(h_ref[...], wbf_ref[...], preferred_element_type=F32)
    o_ref[...] = acc + b_ref[...]


def proj(h, w_stack, b_stack, layer, tm=512, tn=512):
    T, K = h.shape
    N = w_stack.shape[-1]
    vmem = 2 * K * tn * 4 + K * tn * 2 + 2 * tm * K * 2 + 2 * tm * tn * 4 + 8 * MIB
    return pl.pallas_call(
        _proj_body,
        grid=(N // tn, T // tm),
        in_specs=[
            pl.BlockSpec((tm, K), lambda n, m: (m, 0)),
            pl.BlockSpec((None, K, tn), lambda n, m: (layer, 0, n)),
            pl.BlockSpec((None, 1, tn), lambda n, m: (layer, 0, n)),
        ],
        out_specs=pl.BlockSpec((tm, tn), lambda n, m: (m, n)),
        out_shape=jax.ShapeDtypeStruct((T, N), F32),
        scratch_shapes=[pltpu.VMEM((K, tn), BF16)],
        compiler_params=_params(("arbitrary", "arbitrary"), vmem),
        name="in_proj",
    )(h, w_stack, b_stack)


def _out_body(ya_ref, yb_ref, wa_ref, wb_ref, x_ref, o_ref, wa_bf, wb_bf):
    @pl.when(pl.program_id(1) == 0)
    def _():
        wa_bf[...] = wa_ref[...].astype(BF16)
        wb_bf[...] = wb_ref[...].astype(BF16)

    acc = jnp.dot(ya_ref[...], wa_bf[...], preferred_element_type=F32)
    acc = acc + jnp.dot(yb_ref[...], wb_bf[...], preferred_element_type=F32)
    o_ref[...] = x_ref[...] + acc


def out_proj(ya, yb, w_stack, layer, x, tm=512, tn=512):
    T, Kh = ya.shape
    N = w_stack.shape[-1]
    vmem = 2 * (2 * Kh * tn * 4) + 2 * Kh * tn * 2 + 2 * 2 * tm * Kh * 2 + 4 * tm * tn * 4 + 8 * MIB
    return pl.pallas_call(
        _out_body,
        grid=(N // tn, T // tm),
        in_specs=[
            pl.BlockSpec((tm, Kh), lambda n, m: (m, 0)),
            pl.BlockSpec((tm, Kh), lambda n, m: (m, 0)),
            pl.BlockSpec((None, Kh, tn), lambda n, m: (layer, 0, n)),
            pl.BlockSpec((None, Kh, tn), lambda n, m: (layer, 1, n)),
            pl.BlockSpec((tm, tn), lambda n, m: (m, n)),
        ],
        out_specs=pl.BlockSpec((tm, tn), lambda n, m: (m, n)),
        out_shape=jax.ShapeDtypeStruct((T, N), F32),
        scratch_shapes=[pltpu.VMEM((Kh, tn), BF16), pltpu.VMEM((Kh, tn), BF16)],
        compiler_params=_params(("arbitrary", "arbitrary"), vmem),
        name="out_proj",
    )(ya, yb, w_stack, w_stack, x)


def _ffn_up_body(h_ref, wg_ref, wu_ref, o_ref, wg_bf, wu_bf):
    @pl.when(pl.program_id(1) == 0)
    def _():
        wg_bf[...] = wg_ref[...].astype(BF16)
        wu_bf[...] = wu_ref[...].astype(BF16)

    h = h_ref[...]
    g = jnp.dot(h, wg_bf[...], preferred_element_type=F32)
    u = jnp.dot(h, wu_bf[...], preferred_element_type=F32)
    o_ref[...] = ((g * _sigmoid(g)) * u).astype(o_ref.dtype)


def ffn_up(h, wg_stack, wu_stack, layer, tm=512, tn=512):
    T, K = h.shape
    F = wg_stack.shape[-1]
    vmem = 2 * 2 * K * tn * 4 + 2 * K * tn * 2 + 2 * tm * K * 2 + 2 * tm * tn * 2 + 12 * MIB
    return pl.pallas_call(
        _ffn_up_body,
        grid=(F // tn, T // tm),
        in_specs=[
            pl.BlockSpec((tm, K), lambda n, m: (m, 0)),
            pl.BlockSpec((None, K, tn), lambda n, m: (layer, 0, n)),
            pl.BlockSpec((None, K, tn), lambda n, m: (layer, 0, n)),
        ],
        out_specs=pl.BlockSpec((tm, tn), lambda n, m: (m, n)),
        out_shape=jax.ShapeDtypeStruct((T, F), BF16),
        scratch_shapes=[pltpu.VMEM((K, tn), BF16), pltpu.VMEM((K, tn), BF16)],
        compiler_params=_params(("arbitrary", "arbitrary"), vmem),
        name="ffn_up",
    )(h, wg_stack, wu_stack)


def _ffn_down_body(a_ref, w_ref, x_ref, o_ref, wbf_ref):
    @pl.when(pl.program_id(1) == 0)
    def _():
        wbf_ref[...] = w_ref[...].astype(BF16)

    acc = jnp.dot(a_ref[...], wbf_ref[...], preferred_element_type=F32)
    o_ref[...] = x_ref[...] + acc


def ffn_down(a, w_stack, layer, x, tm=512, tn=256):
    T, F = a.shape
    N = w_stack.shape[-1]
    vmem = 2 * F * tn * 4 + F * tn * 2 + 2 * tm * F * 2 + 4 * tm * tn * 4 + 8 * MIB
    return pl.pallas_call(
        _ffn_down_body,
        grid=(N // tn, T // tm),
        in_specs=[
            pl.BlockSpec((tm, F), lambda n, m: (m, 0)),
            pl.BlockSpec((None, F, tn), lambda n, m: (layer, 0, n)),
            pl.BlockSpec((tm, tn), lambda n, m: (m, n)),
        ],
        out_specs=pl.BlockSpec((tm, tn), lambda n, m: (m, n)),
        out_shape=jax.ShapeDtypeStruct((T, N), F32),
        scratch_shapes=[pltpu.VMEM((F, tn), BF16)],
        compiler_params=_params(("arbitrary", "arbitrary"), vmem),
        name="ffn_down",
    )(a, w_stack, x)


def _shift_rows(x, d, fill):
    top = jnp.full((d, x.shape[1]), fill, x.dtype)
    return jnp.concatenate([top, x[: x.shape[0] - d]], axis=0)


def _lru_body(xa_ref, gate_ref, cw_ref, cb_ref, wg_ref, ba_ref, bx_ref, lam_ref, y_ref,
              xe_ref, hc_ref, a_sc, u_sc, *, ts):
    i = pl.program_id(1)
    pad = V7X_SUBLANES
    width = xa_ref.shape[1]

    @pl.when(i == 0)
    def _():
        xe_ref[0:pad, :] = jnp.zeros((pad, width), F32)
        hc_ref[...] = jnp.zeros(hc_ref.shape, F32)
        a_sc[0:pad, :] = jnp.ones((pad, width), F32)
        u_sc[0:pad, :] = jnp.zeros((pad, width), F32)

    xe_ref[pad:pad + ts, :] = xa_ref[...]
    xc = cw_ref[0:1, :] * xe_ref[pl.ds(pad - (LRU_CONV - 1), ts), :]
    for k in range(1, LRU_CONV):
        xc = xc + cw_ref[k:k + 1, :] * xe_ref[pl.ds(pad - (LRU_CONV - 1) + k, ts), :]
    xc = xc + cb_ref[...]
    xe_ref[0:pad, :] = xe_ref[ts:ts + pad, :]

    xcb = xc.astype(BF16)
    r_parts, i_parts = [], []
    for h in range(LRU_HEADS):
        gh = jnp.dot(xcb[:, h * HEAD_DIM:(h + 1) * HEAD_DIM], wg_ref[h].astype(BF16),
                     preferred_element_type=F32)
        r_parts.append(gh[:, :HEAD_DIM])
        i_parts.append(gh[:, HEAD_DIM:])
    r = _sigmoid(jnp.concatenate(r_parts, axis=1) + ba_ref[...])
    ig = _sigmoid(jnp.concatenate(i_parts, axis=1) + bx_ref[...])

    nl = -lam_ref[...]
    softplus = jnp.maximum(nl, 0.0) + jnp.log1p(jnp.exp(-jnp.abs(nl)))
    log_a = (-LRU_C * r) * softplus
    a = jnp.exp(log_a)
    mult = jnp.sqrt(jnp.maximum(jnp.tanh(-log_a) * (a * a + 1.0), 0.0))
    row = lax.broadcasted_iota(jnp.int32, (ts, width), 0)
    mult = jnp.where(jnp.logical_and(i == 0, row == 0), 1.0, mult)
    u = (mult * ig) * xc

    d = 1
    while d < pad:
        a_sc[pad:pad + ts, :] = a
        u_sc[pad:pad + ts, :] = u
        a_s = a_sc[pl.ds(pad - d, ts), :]
        u_s = u_sc[pl.ds(pad - d, ts), :]
        u = a * u_s + u
        a = a * a_s
        d *= 2
    while d < ts:
        u = a * _shift_rows(u, d, 0.0) + u
        a = a * _shift_rows(a, d, 1.0)
        d *= 2
    hseq = a * hc_ref[0:1, :] + u
    hc_ref[0:1, :] = hseq[ts - 1:ts, :]

    g = gate_ref[...]
    c = 0.7978845608028654
    cdf = 0.5 * (1.0 + jnp.tanh(c * (g + 0.044715 * (g * g * g))))
    y_ref[...] = (hseq * (g * cdf)).astype(y_ref.dtype)


def lru_mixer(proj_out, batch, conv_w, conv_b, w_gates, ba, bx, lam, layer, ts=256):
    T = proj_out.shape[0]
    W = LRU_HEADS * HEAD_DIM
    nt = T // batch // ts
    vmem = 2 * 2 * ts * W * 4 + 2 * ts * W * 2 + 3 * (ts + 8) * W * 4 + 24 * ts * W * 4 + 8 * MIB
    vec = pl.BlockSpec((None, 1, W), lambda b, i: (layer, 0, 0))
    return pl.pallas_call(
        functools.partial(_lru_body, ts=ts),
        grid=(batch, nt),
        in_specs=[
            pl.BlockSpec((ts, W), lambda b, i: (b * nt + i, 0)),
            pl.BlockSpec((ts, W), lambda b, i: (b * nt + i, 1)),
            pl.BlockSpec((None, LRU_CONV, W), lambda b, i: (layer, 0, 0)),
            vec,
            pl.BlockSpec((None, LRU_HEADS, HEAD_DIM, 2 * HEAD_DIM), lambda b, i: (layer, 0, 0, 0)),
            vec, vec, vec,
        ],
        out_specs=pl.BlockSpec((ts, W), lambda b, i: (b * nt + i, 0)),
        out_shape=jax.ShapeDtypeStruct((T, W), BF16),
        scratch_shapes=[
            pltpu.VMEM((ts + 8, W), F32),
            pltpu.VMEM((8, W), F32),
            pltpu.VMEM((ts + 8, W), F32),
            pltpu.VMEM((ts + 8, W), F32),
        ],
        compiler_params=_params(("arbitrary", "arbitrary"), min(vmem, 56 * MIB)),
        name="rg_lru",
    )(proj_out, proj_out, conv_w, conv_b, w_gates, ba, bx, lam)


def _cumsum_rows(x):
    n = x.shape[0]
    row = lax.broadcasted_iota(jnp.int32, x.shape, 0)
    d = 1
    while d < n:
        x = x + jnp.where(row >= d, pltpu.roll(x, d, 0), 0.0)
        d *= 2
    return x


def _hgrn_body(q_ref, f_ref, v_ref, g_ref, lbl_ref, ng_ref, y_ref, st_ref, rows_ref, *, ts, layer):
    @pl.when(pl.program_id(1) == 0)
    def _():
        st_ref[...] = jnp.zeros(st_ref.shape, F32)

    logits = lbl_ref[...]
    e = jnp.exp(logits - jnp.max(logits, axis=0, keepdims=True))
    sm = e / jnp.sum(e, axis=0, keepdims=True)
    lb = jnp.sum(sm[0:layer + 1, :], axis=0, keepdims=True) - sm[0:1, :]
    one_m_lb = 1.0 - lb
    ng = ng_ref[...]
    n = HGRN_SUB
    sub = V7X_SUBLANES
    width = q_ref.shape[1]
    subrow = lax.broadcasted_iota(jnp.int32, (sub, width), 0)
    nt_dims = (((1,), (1,)), ((), ()))
    tn_dims = (((0,), (0,)), ((), ()))

    t_idx = lax.broadcasted_iota(jnp.int32, (n, n), 0)
    s_idx = lax.broadcasted_iota(jnp.int32, (n, n), 1)
    diag_mask = t_idx == s_idx
    level_mask = {}
    h = 1
    while h < n:
        level_mask[h] = jnp.logical_and(t_idx // (2 * h) == s_idx // (2 * h),
                                        jnp.logical_and((t_idx // h) % 2 == 1, (s_idx // h) % 2 == 0))
        h *= 2

    def ref_rows(h):
        pieces = []
        for i in range(n // sub):
            base = i * sub
            if 2 * h >= sub:
                rho = (base // (2 * h)) * (2 * h) + h - 1
                pieces.append(jnp.broadcast_to(rows_ref[pl.ds(rho, 1), :], (sub, width)))
            else:
                piece = None
                for c in range(sub // (2 * h)):
                    rho = base + c * 2 * h + h - 1
                    rowv = jnp.broadcast_to(rows_ref[pl.ds(rho, 1), :], (sub, width))
                    piece = rowv if piece is None else jnp.where(subrow >= c * 2 * h, rowv, piece)
                pieces.append(piece)
        return jnp.concatenate(pieces, axis=0)

    def sub_block(j):
        r0 = pl.multiple_of(j * n, n)
        f = f_ref[pl.ds(r0, n), :]
        q = q_ref[pl.ds(r0, n), :]
        v = v_ref[pl.ds(r0, n), :]
        sig = _sigmoid(f)
        fg = jnp.maximum(lb + one_m_lb * sig, F_FLOOR)
        k = one_m_lb * (1.0 - sig)
        qf = q * _sigmoid(q)
        bc = _cumsum_rows(jnp.log(fg)) * LOG2_E
        rows_ref[...] = bc
        b_tot = bc[n - 1:n, :]
        qd = (qf * jnp.exp2(bc)).astype(BF16)
        kd = (k * jnp.exp2(b_tot - bc)).astype(BF16)
        e_tot = jnp.exp2(b_tot)
        vb = v.astype(BF16)
        kb = k.astype(BF16)
        q01 = jnp.concatenate([qf.astype(BF16), (qf * fg).astype(BF16)], axis=0)
        levels = []
        h = n // 2
        while h >= 2:
            br = ref_rows(h)
            ql = (qf * jnp.exp2(jnp.minimum(bc - br, 0.0))).astype(BF16)
            kl = (k * jnp.exp2(jnp.minimum(br - bc, 0.0))).astype(BF16)
            levels.append((h, ql, kl))
            h //= 2

        heads = [slice(hd * HEAD_DIM, (hd + 1) * HEAD_DIM) for hd in range(HGRN_HEADS)]
        ws, inters = [], []
        for hd, sl in enumerate(heads):
            r01 = lax.dot_general(q01[:, sl], kb[:, sl], nt_dims, preferred_element_type=F32)
            w = jnp.where(diag_mask, r01[:n], 0.0) + jnp.where(level_mask[1], r01[n:], 0.0)
            for h, ql, kl in levels:
                r = lax.dot_general(ql[:, sl], kl[:, sl], nt_dims, preferred_element_type=F32)
                w = w + jnp.where(level_mask[h], r, 0.0)
            ws.append(w.astype(BF16))
            st = st_ref[hd]
            inters.append(lax.dot_general(qd[:, sl], st.astype(BF16), nt_dims,
                                          preferred_element_type=F32))
            d_st = lax.dot_general(vb[:, sl], kd[:, sl], tn_dims, preferred_element_type=F32)
            st_ref[hd] = st * e_tot[:, sl] + d_st
        outs = [inters[hd] + jnp.dot(ws[hd], vb[:, sl], preferred_element_type=F32)
                for hd, sl in enumerate(heads)]
        return jnp.concatenate(outs, axis=1)

    def finish(j, o_all):
        r0 = pl.multiple_of(j * n, n)
        g = g_ref[pl.ds(r0, n), :]
        outs = []
        for hd in range(HGRN_HEADS):
            o = o_all[:, hd * HEAD_DIM:(hd + 1) * HEAD_DIM]
            outs.append(o * lax.rsqrt(jnp.mean(o * o, axis=-1, keepdims=True) + EPS))
        o_n = jnp.concatenate(outs, axis=1) * ng
        y_ref[pl.ds(r0, n), :] = (o_n * (g * _sigmoid(g))).astype(y_ref.dtype)

    def step(j, o_prev):
        finish(j - 1, o_prev)
        return sub_block(j)

    n_sub = ts // n
    o_last = lax.fori_loop(1, n_sub, step, sub_block(0))
    finish(n_sub - 1, o_last)


def hgrn_mixer(proj_out, batch, lb_logits, norm_g, layer, ts=256):
    T = proj_out.shape[0]
    W = HGRN_HEADS * HEAD_DIM
    nt = T // batch // ts
    n_layers = lb_logits.shape[0]
    vmem = 2 * 4 * ts * W * 4 + 2 * ts * W * 2 + HGRN_HEADS * HEAD_DIM * HEAD_DIM * 4 + 8 * MIB

    def col(section):
        return pl.BlockSpec((ts, W), lambda b, i: (b * nt + i, section))

    return pl.pallas_call(
        functools.partial(_hgrn_body, ts=ts, layer=layer),
        grid=(batch, nt),
        in_specs=[
            col(2), col(3), col(4), col(5),
            pl.BlockSpec((n_layers, W), lambda b, i: (0, 0)),
            pl.BlockSpec((None, 1, W), lambda b, i: (layer, 0, 0)),
        ],
        out_specs=pl.BlockSpec((ts, W), lambda b, i: (b * nt + i, 0)),
        out_shape=jax.ShapeDtypeStruct((T, W), BF16),
        scratch_shapes=[
            pltpu.VMEM((HGRN_HEADS, HEAD_DIM, HEAD_DIM), F32),
            pltpu.VMEM((HGRN_SUB, W), F32),
        ],
        compiler_params=_params(("arbitrary", "arbitrary"), vmem),
        name="hgrn2",
    )(proj_out, proj_out, proj_out, proj_out, lb_logits, norm_g)


def _odd_body(sb_ref, sc_ref, sv_ref, cu_ref, cg_ref, scw_ref, cfw_ref, cfb_ref, lng_ref, lnb_ref,
              yc_ref, yd_ref, pe_ref, ge_ref, *, ts, rows):
    i = pl.program_id(1)
    width = sb_ref.shape[1]
    pad_p = V7X_SUBLANES
    pad_g = 4 * V7X_SUBLANES

    @pl.when(i == 0)
    def _():
        pe_ref[0:pad_p, :] = jnp.zeros((pad_p, width), F32)
        ge_ref[0:pad_g, :] = jnp.zeros((pad_g, width), F32)

    pe_ref[pad_p:pad_p + ts, :] = sc_ref[...] * sv_ref[...]
    cu = cu_ref[...]
    ge_ref[pad_g:pad_g + ts, :] = cu * _sigmoid(cg_ref[...])

    for c in range(ts // rows):
        r0 = c * rows
        base = pad_p - (SC_CONV - 1) + r0
        acc = scw_ref[0:1, :] * pe_ref[base:base + rows, :]
        for k in range(1, SC_CONV):
            acc = acc + scw_ref[k:k + 1, :] * pe_ref[base + k:base + k + rows, :]
        yc_ref[r0:r0 + rows, :] = (sb_ref[r0:r0 + rows, :] * acc).astype(yc_ref.dtype)

        base = pad_g - (CF_CONV - 1) + r0
        acc = cfw_ref[0:1, :] * ge_ref[base:base + rows, :]
        for k in range(1, CF_CONV):
            acc = acc + cfw_ref[k:k + 1, :] * ge_ref[base + k:base + k + rows, :]
        d = acc + cfb_ref[...]
        mu = jnp.mean(d, axis=-1, keepdims=True)
        dc = d - mu
        var = jnp.mean(dc * dc, axis=-1, keepdims=True)
        z = (dc * lax.rsqrt(var + EPS)) * lng_ref[...] + lnb_ref[...]
        yd_ref[r0:r0 + rows, :] = (z * _sigmoid(z)).astype(yd_ref.dtype)

    pe_ref[0:pad_p, :] = pe_ref[ts:ts + pad_p, :]
    ge_ref[0:pad_g, :] = ge_ref[ts:ts + pad_g, :]


def odd_mixer(proj_out, batch, sc_w, cf_w, cf_b, ln_g, ln_b, layer, ts=256, rows=32):
    T = proj_out.shape[0]
    W = proj_out.shape[1] // 5
    nt = T // batch // ts
    vmem = 2 * 5 * ts * W * 4 + 2 * 2 * ts * W * 2 + 2 * (ts + 32) * W * 4 + 32 * MIB

    def col(section):
        return pl.BlockSpec((ts, W), lambda b, i: (b * nt + i, section))

    vec = pl.BlockSpec((None, 1, W), lambda b, i: (layer, 0, 0))
    out = pl.BlockSpec((ts, W), lambda b, i: (b * nt + i, 0))
    return pl.pallas_call(
        functools.partial(_odd_body, ts=ts, rows=rows),
        grid=(batch, nt),
        in_specs=[
            col(0), col(1), col(2), col(3), col(4),
            pl.BlockSpec((None, SC_CONV, W), lambda b, i: (layer, 0, 0)),
            pl.BlockSpec((None, CF_CONV, W), lambda b, i: (layer, 0, 0)),
            vec, vec, vec,
        ],
        out_specs=[out, out],
        out_shape=[jax.ShapeDtypeStruct((T, W), BF16), jax.ShapeDtypeStruct((T, W), BF16)],
        scratch_shapes=[
            pltpu.VMEM((ts + 8, W), F32),
            pltpu.VMEM((ts + 32, W), F32),
        ],
        compiler_params=_params(("arbitrary", "arbitrary"), vmem),
        name="odd_mixer",
    )(proj_out, proj_out, proj_out, proj_out, proj_out, sc_w, cf_w, cf_b, ln_g, ln_b)


SWEEP_IN = [(512, 512), (1024, 1024), (512, 1024), (256, 1024)]
SWEEP_OUT = [(512, 512), (512, 1024), (1024, 1024), (256, 2048)]
SWEEP_UP = [(512, 512), (1024, 512), (256, 512), (1024, 256)]
SWEEP_DOWN = [(512, 256), (512, 512), (256, 512), (128, 512)]


def kernel(x, ln_mix_g, ln_ffn_g, ln_final_g, ev_w_in, ev_b_in, lru_conv_w, lru_conv_b, lru_wa, lru_ba, lru_wx, lru_bx, lru_lambda, hgrn_lb_logits, hgrn_norm_g, ev_w_out, od_w_in, od_b_in, sc_conv_w, cf_conv_w, cf_conv_b, cf_ln_g, cf_ln_b, od_w_out, ffn_w_gate, ffn_w_up, ffn_w_down):
    B, S, D = x.shape
    depth = ln_mix_g.shape[0]
    xt = x.reshape(B * S, D)

    def rows(p):
        return p.reshape(p.shape[0], 1, p.shape[-1])

    ln_mix = rows(ln_mix_g)
    ln_ffn = rows(ln_ffn_g)
    ln_fin = ln_final_g.reshape(1, 1, D)
    ev_b = rows(ev_b_in)
    od_b = rows(od_b_in)
    w_gates = jnp.concatenate([lru_wa, lru_wx], axis=-1)
    n_even = lru_wa.shape[0]
    lru_ba2 = lru_ba.reshape(n_even, 1, -1)
    lru_bx2 = lru_bx.reshape(n_even, 1, -1)

    for layer in range(depth):
        j = layer // 2
        h = rmsnorm(xt, ln_mix, layer, BF16)
        if layer % 2 == 0:
            p = proj(h, ev_w_in, ev_b, j, *SWEEP_IN[layer])
            ya = lru_mixer(p, B, lru_conv_w, rows(lru_conv_b), w_gates, lru_ba2, lru_bx2,
                           rows(lru_lambda), j)
            yb = hgrn_mixer(p, B, hgrn_lb_logits, rows(hgrn_norm_g), j)
            xt = out_proj(ya, yb, ev_w_out, j, xt, *SWEEP_OUT[layer])
        else:
            p = proj(h, od_w_in, od_b, j, *SWEEP_IN[layer])
            yc, yd = odd_mixer(p, B, sc_conv_w, cf_conv_w, rows(cf_conv_b), rows(cf_ln_g),
                               rows(cf_ln_b), j)
            xt = out_proj(yc, yd, od_w_out, j, xt, *SWEEP_OUT[layer])
        h = rmsnorm(xt, ln_ffn, layer, BF16)
        a = ffn_up(h, ffn_w_gate, ffn_w_up, layer, *SWEEP_UP[layer])
        xt = ffn_down(a, ffn_w_down, layer, xt, *SWEEP_DOWN[layer])
    out = rmsnorm(xt, ln_fin, 0, F32)
    return out.reshape(B, S, D)
```

```python
import functools

import jax
import jax.numpy as jnp
from jax import lax
from jax.experimental import pallas as pl
from jax.experimental.pallas import tpu as pltpu

F32 = jnp.float32
BF16 = jnp.bfloat16

EPS = 1e-6
F_FLOOR = 1e-30
LRU_C = 8.0
LRU_HEADS = 8
LRU_CONV = 4
HGRN_HEADS = 8
HEAD_DIM = 128
SC_CONV = 3
CF_CONV = 31
HGRN_SUB = 16
LOG2_E = 1.4426950408889634

V7X_LANES = 128
V7X_SUBLANES = 8
V7X_VMEM_BYTES = 64 * 2**20
MIB = 2**20
KERNEL_VMEM_LIMIT = V7X_VMEM_BYTES - 8 * MIB

TILE_IN = (1024, 1024)
TILE_OUT = (1024, 1024)
TILE_UP = (1024, 512)
TILE_DOWN = (512, 512)


def _params(*semantics):
    return pltpu.CompilerParams(dimension_semantics=semantics, vmem_limit_bytes=KERNEL_VMEM_LIMIT)


def _sigmoid(x):
    return jax.nn.sigmoid(x)


def _lane_tile(x, width):
    return jnp.concatenate([x] * (width // V7X_LANES), axis=1)


def _sumsq_lanes(x):
    return jnp.broadcast_to(jnp.sum(x * x, axis=-1, keepdims=True), (x.shape[0], V7X_LANES))


def _row_scale(ss_ref, d_model, width):
    ms = jnp.sum(ss_ref[...], axis=0) * (1.0 / d_model)
    return _lane_tile(lax.rsqrt(ms + EPS), width)


def _prenorm_body(x_ref, g_ref, xg_ref, ss_ref):
    x = x_ref[...]
    xg_ref[...] = (x * g_ref[...]).astype(xg_ref.dtype)
    ss_ref[...] = _sumsq_lanes(x)


def prenorm(x, g_stack, layer, tm=512):
    T, D = x.shape
    return pl.pallas_call(
        _prenorm_body,
        grid=(T // tm,),
        in_specs=[
            pl.BlockSpec((tm, D), lambda i: (i, 0)),
            pl.BlockSpec((None, 1, D), lambda i: (layer, 0, 0)),
        ],
        out_specs=[
            pl.BlockSpec((tm, D), lambda i: (i, 0)),
            pl.BlockSpec((None, tm, V7X_LANES), lambda i: (0, i, 0)),
        ],
        out_shape=[jax.ShapeDtypeStruct((T, D), BF16),
                   jax.ShapeDtypeStruct((1, T, V7X_LANES), F32)],
        compiler_params=_params("arbitrary"),
        name="prenorm",
    )(x, g_stack)


def _rmsnorm_body(x_ref, g_ref, o_ref):
    x = x_ref[...]
    ms = jnp.mean(x * x, axis=-1, keepdims=True)
    o_ref[...] = ((x * lax.rsqrt(ms + EPS)) * g_ref[...]).astype(o_ref.dtype)


def rmsnorm(x, g_stack, layer, out_dtype, tm=512):
    T, D = x.shape
    return pl.pallas_call(
        _rmsnorm_body,
        grid=(T // tm,),
        in_specs=[
            pl.BlockSpec((tm, D), lambda i: (i, 0)),
            pl.BlockSpec((None, 1, D), lambda i: (layer, 0, 0)),
        ],
        out_specs=pl.BlockSpec((tm, D), lambda i: (i, 0)),
        out_shape=jax.ShapeDtypeStruct((T, D), out_dtype),
        compiler_params=_params("arbitrary"),
        name="rmsnorm",
    )(x, g_stack)


def _proj_body(xg_ref, ss_ref, w_ref, b_ref, o_ref, wbf_ref, *, d_model):
    @pl.when(pl.program_id(1) == 0)
    def _():
        wbf_ref[...] = w_ref[...].astype(BF16)

    acc = jnp.dot(xg_ref[...], wbf_ref[...], preferred_element_type=F32)
    o_ref[...] = _row_scale(ss_ref, d_model, acc.shape[1]) * acc + b_ref[...]


def proj(xg, ss, w_stack, b_stack, layer):
    tm, tn = TILE_IN
    T, K = xg.shape
    N = w_stack.shape[-1]
    parts = ss.shape[0]
    return pl.pallas_call(
        functools.partial(_proj_body, d_model=K),
        grid=(N // tn, T // tm),
        in_specs=[
            pl.BlockSpec((tm, K), lambda n, m: (m, 0)),
            pl.BlockSpec((parts, tm, V7X_LANES), lambda n, m: (0, m, 0)),
            pl.BlockSpec((None, K, tn), lambda n, m: (layer, 0, n)),
            pl.BlockSpec((None, 1, tn), lambda n, m: (layer, 0, n)),
        ],
        out_specs=pl.BlockSpec((tm, tn), lambda n, m: (m, n)),
        out_shape=jax.ShapeDtypeStruct((T, N), F32),
        scratch_shapes=[pltpu.VMEM((K, tn), BF16)],
        compiler_params=_params("arbitrary", "arbitrary"),
        name="in_proj",
    )(xg, ss, w_stack, b_stack)


def _emit_residual(xn, gn_ref, o_ref, xg_ref, ss_ref):
    o_ref[...] = xn
    if gn_ref is not None:
        xg_ref[...] = (xn * gn_ref[...]).astype(xg_ref.dtype)
        ss_ref[...] = _sumsq_lanes(xn)


def _out_body(*refs, emit_norm):
    if emit_norm:
        ya_ref, yb_ref, wa_ref, wb_ref, x_ref, gn_ref, o_ref, xg_ref, ss_ref, wa_bf, wb_bf = refs
    else:
        ya_ref, yb_ref, wa_ref, wb_ref, x_ref, o_ref, wa_bf, wb_bf = refs
        gn_ref = xg_ref = ss_ref = None

    @pl.when(pl.program_id(1) == 0)
    def _():
        wa_bf[...] = wa_ref[...].astype(BF16)
        wb_bf[...] = wb_ref[...].astype(BF16)

    acc = jnp.dot(ya_ref[...], wa_bf[...], preferred_element_type=F32)
    acc = acc + jnp.dot(yb_ref[...], wb_bf[...], preferred_element_type=F32)
    _emit_residual(x_ref[...] + acc, gn_ref, o_ref, xg_ref, ss_ref)


def _residual_specs(T, N, tm, tn, next_gain, next_layer):
    x_spec = pl.BlockSpec((tm, tn), lambda n, m: (m, n))
    if next_gain is None:
        return [], [], x_spec, jax.ShapeDtypeStruct((T, N), F32)
    ins = [pl.BlockSpec((None, 1, tn), lambda n, m: (next_layer, 0, n))]
    outs = [x_spec, x_spec, pl.BlockSpec((None, tm, V7X_LANES), lambda n, m: (n, m, 0))]
    shapes = [jax.ShapeDtypeStruct((T, N), F32), jax.ShapeDtypeStruct((T, N), BF16),
              jax.ShapeDtypeStruct((N // tn, T, V7X_LANES), F32)]
    return ins, [next_gain], outs, shapes


def out_proj(ya, yb, w_stack, layer, x, next_gain, next_layer):
    tm, tn = TILE_OUT
    T, Kh = ya.shape
    N = w_stack.shape[-1]
    g_in, g_arg, outs, shapes = _residual_specs(T, N, tm, tn, next_gain, next_layer)
    return pl.pallas_call(
        functools.partial(_out_body, emit_norm=next_gain is not None),
        grid=(N // tn, T // tm),
        in_specs=[
            pl.BlockSpec((tm, Kh), lambda n, m: (m, 0)),
            pl.BlockSpec((tm, Kh), lambda n, m: (m, 0)),
            pl.BlockSpec((None, Kh, tn), lambda n, m: (layer, 0, n)),
            pl.BlockSpec((None, Kh, tn), lambda n, m: (layer, 1, n)),
            pl.BlockSpec((tm, tn), lambda n, m: (m, n)),
        ] + g_in,
        out_specs=outs,
        out_shape=shapes,
        scratch_shapes=[pltpu.VMEM((Kh, tn), BF16), pltpu.VMEM((Kh, tn), BF16)],
        compiler_params=_params("arbitrary", "arbitrary"),
        name="out_proj",
    )(ya, yb, w_stack, w_stack, x, *g_arg)


def _ffn_up_body(xg_ref, ss_ref, wg_ref, wu_ref, o_ref, wg_bf, wu_bf, *, d_model):
    @pl.when(pl.program_id(1) == 0)
    def _():
        wg_bf[...] = wg_ref[...].astype(BF16)
        wu_bf[...] = wu_ref[...].astype(BF16)

    h = xg_ref[...]
    rs = _row_scale(ss_ref, d_model, o_ref.shape[1])
    g = rs * jnp.dot(h, wg_bf[...], preferred_element_type=F32)
    u = rs * jnp.dot(h, wu_bf[...], preferred_element_type=F32)
    o_ref[...] = ((g * _sigmoid(g)) * u).astype(o_ref.dtype)


def ffn_up(xg, ss, wg_stack, wu_stack, layer):
    tm, tn = TILE_UP
    T, K = xg.shape
    F = wg_stack.shape[-1]
    parts = ss.shape[0]
    return pl.pallas_call(
        functools.partial(_ffn_up_body, d_model=K),
        grid=(F // tn, T // tm),
        in_specs=[
            pl.BlockSpec((tm, K), lambda n, m: (m, 0)),
            pl.BlockSpec((parts, tm, V7X_LANES), lambda n, m: (0, m, 0)),
            pl.BlockSpec((None, K, tn), lambda n, m: (layer, 0, n)),
            pl.BlockSpec((None, K, tn), lambda n, m: (layer, 0, n)),
        ],
        out_specs=pl.BlockSpec((tm, tn), lambda n, m: (m, n)),
        out_shape=jax.ShapeDtypeStruct((T, F), BF16),
        scratch_shapes=[pltpu.VMEM((K, tn), BF16), pltpu.VMEM((K, tn), BF16)],
        compiler_params=_params("arbitrary", "arbitrary"),
        name="ffn_up",
    )(xg, ss, wg_stack, wu_stack)


def _ffn_down_body(*refs, emit_norm):
    if emit_norm:
        a_ref, w_ref, x_ref, gn_ref, o_ref, xg_ref, ss_ref, wbf_ref = refs
    else:
        a_ref, w_ref, x_ref, o_ref, wbf_ref = refs
        gn_ref = xg_ref = ss_ref = None

    @pl.when(pl.program_id(1) == 0)
    def _():
        wbf_ref[...] = w_ref[...].astype(BF16)

    acc = jnp.dot(a_ref[...], wbf_ref[...], preferred_element_type=F32)
    _emit_residual(x_ref[...] + acc, gn_ref, o_ref, xg_ref, ss_ref)


def ffn_down(a, w_stack, layer, x, next_gain, next_layer):
    tm, tn = TILE_DOWN
    T, F = a.shape
    N = w_stack.shape[-1]
    g_in, g_arg, outs, shapes = _residual_specs(T, N, tm, tn, next_gain, next_layer)
    return pl.pallas_call(
        functools.partial(_ffn_down_body, emit_norm=next_gain is not None),
        grid=(N // tn, T // tm),
        in_specs=[
            pl.BlockSpec((tm, F), lambda n, m: (m, 0)),
            pl.BlockSpec((None, F, tn), lambda n, m: (layer, 0, n)),
            pl.BlockSpec((tm, tn), lambda n, m: (m, n)),
        ] + g_in,
        out_specs=outs,
        out_shape=shapes,
        scratch_shapes=[pltpu.VMEM((F, tn), BF16)],
        compiler_params=_params("arbitrary", "arbitrary"),
        name="ffn_down",
    )(a, w_stack, x, *g_arg)


def _shift_rows(x, d, fill):
    top = jnp.full((d, x.shape[1]), fill, x.dtype)
    return jnp.concatenate([top, x[: x.shape[0] - d]], axis=0)


def _lru_body(xa_ref, gate_ref, cw_ref, cb_ref, wg_ref, ba_ref, bx_ref, lam_ref, y_ref,
              xe_ref, hc_ref, a_sc, u_sc, *, ts):
    i = pl.program_id(1)
    pad = V7X_SUBLANES
    width = xa_ref.shape[1]

    @pl.when(i == 0)
    def _():
        xe_ref[0:pad, :] = jnp.zeros((pad, width), F32)
        hc_ref[...] = jnp.zeros(hc_ref.shape, F32)
        a_sc[0:pad, :] = jnp.ones((pad, width), F32)
        u_sc[0:pad, :] = jnp.zeros((pad, width), F32)

    xe_ref[pad:pad + ts, :] = xa_ref[...]
    xc = cw_ref[0:1, :] * xe_ref[pl.ds(pad - (LRU_CONV - 1), ts), :]
    for k in range(1, LRU_CONV):
        xc = xc + cw_ref[k:k + 1, :] * xe_ref[pl.ds(pad - (LRU_CONV - 1) + k, ts), :]
    xc = xc + cb_ref[...]
    xe_ref[0:pad, :] = xe_ref[ts:ts + pad, :]

    xcb = xc.astype(BF16)
    r_parts, i_parts = [], []
    for h in range(LRU_HEADS):
        gh = jnp.dot(xcb[:, h * HEAD_DIM:(h + 1) * HEAD_DIM], wg_ref[h].astype(BF16),
                     preferred_element_type=F32)
        r_parts.append(gh[:, :HEAD_DIM])
        i_parts.append(gh[:, HEAD_DIM:])
    r = _sigmoid(jnp.concatenate(r_parts, axis=1) + ba_ref[...])
    ig = _sigmoid(jnp.concatenate(i_parts, axis=1) + bx_ref[...])

    nl = -lam_ref[...]
    softplus = jnp.maximum(nl, 0.0) + jnp.log1p(jnp.exp(-jnp.abs(nl)))
    log_a = (-LRU_C * r) * softplus
    a = jnp.exp(log_a)
    mult = jnp.sqrt(jnp.maximum(jnp.tanh(-log_a) * (a * a + 1.0), 0.0))
    row = lax.broadcasted_iota(jnp.int32, (ts, width), 0)
    mult = jnp.where(jnp.logical_and(i == 0, row == 0), 1.0, mult)
    u = (mult * ig) * xc

    d = 1
    while d < pad:
        a_sc[pad:pad + ts, :] = a
        u_sc[pad:pad + ts, :] = u
        a_s = a_sc[pl.ds(pad - d, ts), :]
        u_s = u_sc[pl.ds(pad - d, ts), :]
        u = a * u_s + u
        a = a * a_s
        d *= 2
    while d < ts:
        u = a * _shift_rows(u, d, 0.0) + u
        a = a * _shift_rows(a, d, 1.0)
        d *= 2
    hseq = a * hc_ref[0:1, :] + u
    hc_ref[0:1, :] = hseq[ts - 1:ts, :]

    g = gate_ref[...]
    c = 0.7978845608028654
    cdf = 0.5 * (1.0 + jnp.tanh(c * (g + 0.044715 * (g * g * g))))
    y_ref[...] = (hseq * (g * cdf)).astype(y_ref.dtype)


def lru_mixer(proj_out, batch, conv_w, conv_b, w_gates, ba, bx, lam, layer, ts=256):
    T = proj_out.shape[0]
    W = LRU_HEADS * HEAD_DIM
    nt = T // batch // ts
    vec = pl.BlockSpec((None, 1, W), lambda b, i: (layer, 0, 0))
    return pl.pallas_call(
        functools.partial(_lru_body, ts=ts),
        grid=(batch, nt),
        in_specs=[
            pl.BlockSpec((ts, W), lambda b, i: (b * nt + i, 0)),
            pl.BlockSpec((ts, W), lambda b, i: (b * nt + i, 1)),
            pl.BlockSpec((None, LRU_CONV, W), lambda b, i: (layer, 0, 0)),
            vec,
            pl.BlockSpec((None, LRU_HEADS, HEAD_DIM, 2 * HEAD_DIM), lambda b, i: (layer, 0, 0, 0)),
            vec, vec, vec,
        ],
        out_specs=pl.BlockSpec((ts, W), lambda b, i: (b * nt + i, 0)),
        out_shape=jax.ShapeDtypeStruct((T, W), BF16),
        scratch_shapes=[
            pltpu.VMEM((ts + 8, W), F32),
            pltpu.VMEM((8, W), F32),
            pltpu.VMEM((ts + 8, W), F32),
            pltpu.VMEM((ts + 8, W), F32),
        ],
        compiler_params=_params("arbitrary", "arbitrary"),
        name="rg_lru",
    )(proj_out, proj_out, conv_w, conv_b, w_gates, ba, bx, lam)


def _cumsum_rows(x):
    n = x.shape[0]
    row = lax.broadcasted_iota(jnp.int32, x.shape, 0)
    d = 1
    while d < n:
        x = x + jnp.where(row >= d, pltpu.roll(x, d, 0), 0.0)
        d *= 2
    return x


def _hgrn_body(q_ref, f_ref, v_ref, g_ref, lbl_ref, ng_ref, y_ref, st_ref, rows_ref, *, ts, layer):
    @pl.when(pl.program_id(1) == 0)
    def _():
        st_ref[...] = jnp.zeros(st_ref.shape, F32)

    logits = lbl_ref[...]
    e = jnp.exp(logits - jnp.max(logits, axis=0, keepdims=True))
    sm = e / jnp.sum(e, axis=0, keepdims=True)
    lb = jnp.sum(sm[0:layer + 1, :], axis=0, keepdims=True) - sm[0:1, :]
    one_m_lb = 1.0 - lb
    ng = ng_ref[...]
    n = HGRN_SUB
    sub = V7X_SUBLANES
    width = q_ref.shape[1]
    subrow = lax.broadcasted_iota(jnp.int32, (sub, width), 0)
    nt_dims = (((1,), (1,)), ((), ()))
    tn_dims = (((0,), (0,)), ((), ()))

    t_idx = lax.broadcasted_iota(jnp.int32, (n, n), 0)
    s_idx = lax.broadcasted_iota(jnp.int32, (n, n), 1)
    diag_mask = t_idx == s_idx
    level_mask = {}
    h = 1
    while h < n:
        level_mask[h] = jnp.logical_and(t_idx // (2 * h) == s_idx // (2 * h),
                                        jnp.logical_and((t_idx // h) % 2 == 1, (s_idx // h) % 2 == 0))
        h *= 2

    def ref_rows(h):
        pieces = []
        for i in range(n // sub):
            base = i * sub
            if 2 * h >= sub:
                rho = (base // (2 * h)) * (2 * h) + h - 1
                pieces.append(jnp.broadcast_to(rows_ref[pl.ds(rho, 1), :], (sub, width)))
            else:
                piece = None
                for c in range(sub // (2 * h)):
                    rho = base + c * 2 * h + h - 1
                    rowv = jnp.broadcast_to(rows_ref[pl.ds(rho, 1), :], (sub, width))
                    piece = rowv if piece is None else jnp.where(subrow >= c * 2 * h, rowv, piece)
                pieces.append(piece)
        return jnp.concatenate(pieces, axis=0)

    def sub_block(j):
        r0 = pl.multiple_of(j * n, n)
        f = f_ref[pl.ds(r0, n), :]
        q = q_ref[pl.ds(r0, n), :]
        v = v_ref[pl.ds(r0, n), :]
        sig = _sigmoid(f)
        fg = jnp.maximum(lb + one_m_lb * sig, F_FLOOR)
        k = one_m_lb * (1.0 - sig)
        qf = q * _sigmoid(q)
        bc = _cumsum_rows(jnp.log(fg)) * LOG2_E
        rows_ref[...] = bc
        b_tot = bc[n - 1:n, :]
        qd = (qf * jnp.exp2(bc)).astype(BF16)
        kd = (k * jnp.exp2(b_tot - bc)).astype(BF16)
        e_tot = jnp.exp2(b_tot)
        vb = v.astype(BF16)
        kb = k.astype(BF16)
        q01 = jnp.concatenate([qf.astype(BF16), (qf * fg).astype(BF16)], axis=0)
        levels = []
        h = n // 2
        while h >= 2:
            br = ref_rows(h)
            ql = (qf * jnp.exp2(jnp.minimum(bc - br, 0.0))).astype(BF16)
            kl = (k * jnp.exp2(jnp.minimum(br - bc, 0.0))).astype(BF16)
            levels.append((h, ql, kl))
            h //= 2

        heads = [slice(hd * HEAD_DIM, (hd + 1) * HEAD_DIM) for hd in range(HGRN_HEADS)]
        ws, inters = [], []
        for hd, sl in enumerate(heads):
            r01 = lax.dot_general(q01[:, sl], kb[:, sl], nt_dims, preferred_element_type=F32)
            w = jnp.where(diag_mask, r01[:n], 0.0) + jnp.where(level_mask[1], r01[n:], 0.0)
            for h, ql, kl in levels:
                r = lax.dot_general(ql[:, sl], kl[:, sl], nt_dims, preferred_element_type=F32)
                w = w + jnp.where(level_mask[h], r, 0.0)
            ws.append(w.astype(BF16))
            st = st_ref[hd]
            inters.append(lax.dot_general(qd[:, sl], st.astype(BF16), nt_dims,
                                          preferred_element_type=F32))
            d_st = lax.dot_general(vb[:, sl], kd[:, sl], tn_dims, preferred_element_type=F32)
            st_ref[hd] = st * e_tot[:, sl] + d_st
        outs = [inters[hd] + jnp.dot(ws[hd], vb[:, sl], preferred_element_type=F32)
                for hd, sl in enumerate(heads)]
        return jnp.concatenate(outs, axis=1)

    def finish(j, o_all):
        r0 = pl.multiple_of(j * n, n)
        g = g_ref[pl.ds(r0, n), :]
        outs = []
        for hd in range(HGRN_HEADS):
            o = o_all[:, hd * HEAD_DIM:(hd + 1) * HEAD_DIM]
            outs.append(o * lax.rsqrt(jnp.mean(o * o, axis=-1, keepdims=True) + EPS))
        o_n = jnp.concatenate(outs, axis=1) * ng
        y_ref[pl.ds(r0, n), :] = (o_n * (g * _sigmoid(g))).astype(y_ref.dtype)

    def step(j, o_prev):
        finish(j - 1, o_prev)
        return sub_block(j)

    n_sub = ts // n
    o_last = lax.fori_loop(1, n_sub, step, sub_block(0))
    finish(n_sub - 1, o_last)


def hgrn_mixer(proj_out, batch, lb_logits, norm_g, layer, ts=256):
    T = proj_out.shape[0]
    W = HGRN_HEADS * HEAD_DIM
    nt = T // batch // ts
    n_layers = lb_logits.shape[0]

    def col(section):
        return pl.BlockSpec((ts, W), lambda b, i: (b * nt + i, section))

    return pl.pallas_call(
        functools.partial(_hgrn_body, ts=ts, layer=layer),
        grid=(batch, nt),
        in_specs=[
            col(2), col(3), col(4), col(5),
            pl.BlockSpec((n_layers, W), lambda b, i: (0, 0)),
            pl.BlockSpec((None, 1, W), lambda b, i: (layer, 0, 0)),
        ],
        out_specs=pl.BlockSpec((ts, W), lambda b, i: (b * nt + i, 0)),
        out_shape=jax.ShapeDtypeStruct((T, W), BF16),
        scratch_shapes=[
            pltpu.VMEM((HGRN_HEADS, HEAD_DIM, HEAD_DIM), F32),
            pltpu.VMEM((HGRN_SUB, W), F32),
        ],
        compiler_params=_params("arbitrary", "arbitrary"),
        name="hgrn2",
    )(proj_out, proj_out, proj_out, proj_out, lb_logits, norm_g)


def _odd_body(sb_ref, sc_ref, sv_ref, cu_ref, cg_ref, scw_ref, cfw_ref, cfb_ref, lng_ref, lnb_ref,
              yc_ref, yd_ref, pe_ref, ge_ref, d_ref, *, ts, rows, conv_rows, conv_lanes):
    i = pl.program_id(1)
    width = sb_ref.shape[1]
    sub = V7X_SUBLANES
    pad_p = sub
    pad_g = 4 * sub

    @pl.when(i == 0)
    def _():
        pe_ref[0:pad_p, :] = jnp.zeros((pad_p, width), F32)
        ge_ref[0:pad_g, :] = jnp.zeros((pad_g, width), F32)

    pe_ref[pad_p:pad_p + ts, :] = sc_ref[...] * sv_ref[...]
    ge_ref[pad_g:pad_g + ts, :] = cu_ref[...] * _sigmoid(cg_ref[...])

    phases = {}
    for k in range(CF_CONV):
        off = pad_g - (CF_CONV - 1) + k
        phases.setdefault(off % sub, []).append((k, off - off % sub))
    for r0 in range(0, ts, conv_rows):
        for c0 in range(0, width, conv_lanes):
            cols = slice(c0, c0 + conv_lanes)
            acc = None
            for p, taps in sorted(phases.items()):
                nrows = conv_rows if p == 0 else conv_rows + sub
                z = None
                for k, q in taps:
                    term = cfw_ref[k:k + 1, cols] * ge_ref[r0 + q:r0 + q + nrows, cols]
                    z = term if z is None else z + term
                part = z if p == 0 else z[p:p + conv_rows]
                acc = part if acc is None else acc + part
            d_ref[r0:r0 + conv_rows, cols] = acc + cfb_ref[:, cols]

    for r0 in range(0, ts, rows):
        base = pad_p - (SC_CONV - 1) + r0
        acc = scw_ref[0:1, :] * pe_ref[base:base + rows, :]
        for k in range(1, SC_CONV):
            acc = acc + scw_ref[k:k + 1, :] * pe_ref[base + k:base + k + rows, :]
        yc_ref[r0:r0 + rows, :] = (sb_ref[r0:r0 + rows, :] * acc).astype(yc_ref.dtype)

        d = d_ref[r0:r0 + rows, :]
        mu = jnp.mean(d, axis=-1, keepdims=True)
        dc = d - mu
        var = jnp.mean(dc * dc, axis=-1, keepdims=True)
        z = (dc * lax.rsqrt(var + EPS)) * lng_ref[...] + lnb_ref[...]
        yd_ref[r0:r0 + rows, :] = (z * _sigmoid(z)).astype(yd_ref.dtype)

    pe_ref[0:pad_p, :] = pe_ref[ts:ts + pad_p, :]
    ge_ref[0:pad_g, :] = ge_ref[ts:ts + pad_g, :]


def odd_mixer(proj_out, batch, sc_w, cf_w, cf_b, ln_g, ln_b, layer, ts=256, rows=32,
              conv_rows=64, conv_lanes=256):
    T = proj_out.shape[0]
    W = proj_out.shape[1] // 5
    nt = T // batch // ts

    def col(section):
        return pl.BlockSpec((ts, W), lambda b, i: (b * nt + i, section))

    vec = pl.BlockSpec((None, 1, W), lambda b, i: (layer, 0, 0))
    out = pl.BlockSpec((ts, W), lambda b, i: (b * nt + i, 0))
    return pl.pallas_call(
        functools.partial(_odd_body, ts=ts, rows=rows, conv_rows=conv_rows, conv_lanes=conv_lanes),
        grid=(batch, nt),
        in_specs=[
            col(0), col(1), col(2), col(3), col(4),
            pl.BlockSpec((None, SC_CONV, W), lambda b, i: (layer, 0, 0)),
            pl.BlockSpec((None, CF_CONV, W), lambda b, i: (layer, 0, 0)),
            vec, vec, vec,
        ],
        out_specs=[out, out],
        out_shape=[jax.ShapeDtypeStruct((T, W), BF16), jax.ShapeDtypeStruct((T, W), BF16)],
        scratch_shapes=[
            pltpu.VMEM((ts + 8, W), F32),
            pltpu.VMEM((ts + 32, W), F32),
            pltpu.VMEM((ts, W), F32),
        ],
        compiler_params=_params("arbitrary", "arbitrary"),
        name="odd_mixer",
    )(proj_out, proj_out, proj_out, proj_out, proj_out, sc_w, cf_w, cf_b, ln_g, ln_b)


def kernel(x, ln_mix_g, ln_ffn_g, ln_final_g, ev_w_in, ev_b_in, lru_conv_w, lru_conv_b, lru_wa, lru_ba, lru_wx, lru_bx, lru_lambda, hgrn_lb_logits, hgrn_norm_g, ev_w_out, od_w_in, od_b_in, sc_conv_w, cf_conv_w, cf_conv_b, cf_ln_g, cf_ln_b, od_w_out, ffn_w_gate, ffn_w_up, ffn_w_down):
    B, S, D = x.shape
    depth = ln_mix_g.shape[0]
    xt = x.reshape(B * S, D)

    def rows(p):
        return p.reshape(p.shape[0], 1, p.shape[-1])

    ln_mix = rows(ln_mix_g)
    ln_ffn = rows(ln_ffn_g)
    ln_fin = ln_final_g.reshape(1, 1, D)
    ev_b = rows(ev_b_in)
    od_b = rows(od_b_in)
    w_gates = jnp.concatenate([lru_wa, lru_wx], axis=-1)
    n_even = lru_wa.shape[0]
    lru_ba2 = lru_ba.reshape(n_even, 1, -1)
    lru_bx2 = lru_bx.reshape(n_even, 1, -1)

    xg, ss = prenorm(xt, ln_mix, 0)
    for layer in range(depth):
        j = layer // 2
        if layer % 2 == 0:
            p = proj(xg, ss, ev_w_in, ev_b, j)
            ya = lru_mixer(p, B, lru_conv_w, rows(lru_conv_b), w_gates, lru_ba2, lru_bx2,
                           rows(lru_lambda), j)
            yb = hgrn_mixer(p, B, hgrn_lb_logits, rows(hgrn_norm_g), j)
            xt, xg, ss = out_proj(ya, yb, ev_w_out, j, xt, ln_ffn, layer)
        else:
            p = proj(xg, ss, od_w_in, od_b, j)
            yc, yd = odd_mixer(p, B, sc_conv_w, cf_conv_w, rows(cf_conv_b), rows(cf_ln_g),
                               rows(cf_ln_b), j)
            xt, xg, ss = out_proj(yc, yd, od_w_out, j, xt, ln_ffn, layer)
        a = ffn_up(xg, ss, ffn_w_gate, ffn_w_up, layer)
        if layer + 1 < depth:
            xt, xg, ss = ffn_down(a, ffn_w_down, layer, xt, ln_mix, layer + 1)
        else:
            xt = ffn_down(a, ffn_w_down, layer, xt, None, 0)
    out = rmsnorm(xt, ln_fin, 0, F32)
    return out.reshape(B, S, D)
```

```python
import functools

import jax
import jax.numpy as jnp
from jax import lax
from jax.experimental import pallas as pl
from jax.experimental.pallas import tpu as pltpu

F32 = jnp.float32
BF16 = jnp.bfloat16

EPS = 1e-6
F_FLOOR = 1e-30
LRU_C = 8.0
LRU_HEADS = 8
LRU_CONV = 4
HGRN_HEADS = 8
HEAD_DIM = 128
SC_CONV = 3
CF_CONV = 31
HGRN_SUB = 32
LOG2_E = 1.4426950408889634

V7X_LANES = 128
V7X_SUBLANES = 8
V7X_VMEM_BYTES = 64 * 2**20
MIB = 2**20
KERNEL_VMEM_LIMIT = V7X_VMEM_BYTES - 8 * MIB

TILE_IN = (1024, 1024)
TILE_OUT = (1024, 1024)
TILE_UP = (1024, 512)
TILE_DOWN = (512, 512)


def _params(*semantics):
    return pltpu.CompilerParams(dimension_semantics=semantics, vmem_limit_bytes=KERNEL_VMEM_LIMIT)


def _sigmoid(x):
    return jax.nn.sigmoid(x)


def _lane_tile(x, width):
    return jnp.concatenate([x] * (width // V7X_LANES), axis=1)


def _sumsq_lanes(x):
    return jnp.broadcast_to(jnp.sum(x * x, axis=-1, keepdims=True), (x.shape[0], V7X_LANES))


def _row_scale(ss_ref, d_model, width):
    ms = jnp.sum(ss_ref[...], axis=0) * (1.0 / d_model)
    return _lane_tile(lax.rsqrt(ms + EPS), width)


def _prenorm_body(x_ref, g_ref, xg_ref, ss_ref):
    x = x_ref[...]
    xg_ref[...] = (x * g_ref[...]).astype(xg_ref.dtype)
    ss_ref[...] = _sumsq_lanes(x)


def prenorm(x, g_stack, layer, tm=512):
    T, D = x.shape
    return pl.pallas_call(
        _prenorm_body,
        grid=(T // tm,),
        in_specs=[
            pl.BlockSpec((tm, D), lambda i: (i, 0)),
            pl.BlockSpec((None, 1, D), lambda i: (layer, 0, 0)),
        ],
        out_specs=[
            pl.BlockSpec((tm, D), lambda i: (i, 0)),
            pl.BlockSpec((None, tm, V7X_LANES), lambda i: (0, i, 0)),
        ],
        out_shape=[jax.ShapeDtypeStruct((T, D), BF16),
                   jax.ShapeDtypeStruct((1, T, V7X_LANES), F32)],
        compiler_params=_params("arbitrary"),
        name="prenorm",
    )(x, g_stack)


def _rmsnorm_body(x_ref, g_ref, o_ref):
    x = x_ref[...]
    ms = jnp.mean(x * x, axis=-1, keepdims=True)
    o_ref[...] = ((x * lax.rsqrt(ms + EPS)) * g_ref[...]).astype(o_ref.dtype)


def rmsnorm(x, g_stack, layer, out_dtype, tm=512):
    T, D = x.shape
    return pl.pallas_call(
        _rmsnorm_body,
        grid=(T // tm,),
        in_specs=[
            pl.BlockSpec((tm, D), lambda i: (i, 0)),
            pl.BlockSpec((None, 1, D), lambda i: (layer, 0, 0)),
        ],
        out_specs=pl.BlockSpec((tm, D), lambda i: (i, 0)),
        out_shape=jax.ShapeDtypeStruct((T, D), out_dtype),
        compiler_params=_params("arbitrary"),
        name="rmsnorm",
    )(x, g_stack)


def _proj_body(xg_ref, ss_ref, w_ref, b_ref, o_ref, wbf_ref, *, d_model):
    @pl.when(pl.program_id(1) == 0)
    def _():
        wbf_ref[...] = w_ref[...].astype(BF16)

    acc = jnp.dot(xg_ref[...], wbf_ref[...], preferred_element_type=F32)
    o_ref[...] = _row_scale(ss_ref, d_model, acc.shape[1]) * acc + b_ref[...]


def proj(xg, ss, w_stack, b_stack, layer):
    tm, tn = TILE_IN
    T, K = xg.shape
    N = w_stack.shape[-1]
    parts = ss.shape[0]
    return pl.pallas_call(
        functools.partial(_proj_body, d_model=K),
        grid=(N // tn, T // tm),
        in_specs=[
            pl.BlockSpec((tm, K), lambda n, m: (m, 0)),
            pl.BlockSpec((parts, tm, V7X_LANES), lambda n, m: (0, m, 0)),
            pl.BlockSpec((None, K, tn), lambda n, m: (layer, 0, n)),
            pl.BlockSpec((None, 1, tn), lambda n, m: (layer, 0, n)),
        ],
        out_specs=pl.BlockSpec((tm, tn), lambda n, m: (m, n)),
        out_shape=jax.ShapeDtypeStruct((T, N), F32),
        scratch_shapes=[pltpu.VMEM((K, tn), BF16)],
        compiler_params=_params("arbitrary", "arbitrary"),
        name="in_proj",
    )(xg, ss, w_stack, b_stack)


def _emit_residual(xn, gn_ref, o_ref, xg_ref, ss_ref):
    o_ref[...] = xn
    if gn_ref is not None:
        xg_ref[...] = (xn * gn_ref[...]).astype(xg_ref.dtype)
        ss_ref[...] = _sumsq_lanes(xn)


def _out_body(*refs, emit_norm):
    if emit_norm:
        ya_ref, yb_ref, wa_ref, wb_ref, x_ref, gn_ref, o_ref, xg_ref, ss_ref, wa_bf, wb_bf = refs
    else:
        ya_ref, yb_ref, wa_ref, wb_ref, x_ref, o_ref, wa_bf, wb_bf = refs
        gn_ref = xg_ref = ss_ref = None

    @pl.when(pl.program_id(1) == 0)
    def _():
        wa_bf[...] = wa_ref[...].astype(BF16)
        wb_bf[...] = wb_ref[...].astype(BF16)

    acc = jnp.dot(ya_ref[...], wa_bf[...], preferred_element_type=F32)
    acc = acc + jnp.dot(yb_ref[...], wb_bf[...], preferred_element_type=F32)
    _emit_residual(x_ref[...] + acc, gn_ref, o_ref, xg_ref, ss_ref)


def _residual_specs(T, N, tm, tn, next_gain, next_layer):
    x_spec = pl.BlockSpec((tm, tn), lambda n, m: (m, n))
    if next_gain is None:
        return [], [], x_spec, jax.ShapeDtypeStruct((T, N), F32)
    ins = [pl.BlockSpec((None, 1, tn), lambda n, m: (next_layer, 0, n))]
    outs = [x_spec, x_spec, pl.BlockSpec((None, tm, V7X_LANES), lambda n, m: (n, m, 0))]
    shapes = [jax.ShapeDtypeStruct((T, N), F32), jax.ShapeDtypeStruct((T, N), BF16),
              jax.ShapeDtypeStruct((N // tn, T, V7X_LANES), F32)]
    return ins, [next_gain], outs, shapes


def out_proj(ya, yb, w_stack, layer, x, next_gain, next_layer):
    tm, tn = TILE_OUT
    T, Kh = ya.shape
    N = w_stack.shape[-1]
    g_in, g_arg, outs, shapes = _residual_specs(T, N, tm, tn, next_gain, next_layer)
    return pl.pallas_call(
        functools.partial(_out_body, emit_norm=next_gain is not None),
        grid=(N // tn, T // tm),
        in_specs=[
            pl.BlockSpec((tm, Kh), lambda n, m: (m, 0)),
            pl.BlockSpec((tm, Kh), lambda n, m: (m, 0)),
            pl.BlockSpec((None, Kh, tn), lambda n, m: (layer, 0, n)),
            pl.BlockSpec((None, Kh, tn), lambda n, m: (layer, 1, n)),
            pl.BlockSpec((tm, tn), lambda n, m: (m, n)),
        ] + g_in,
        out_specs=outs,
        out_shape=shapes,
        scratch_shapes=[pltpu.VMEM((Kh, tn), BF16), pltpu.VMEM((Kh, tn), BF16)],
        compiler_params=_params("arbitrary", "arbitrary"),
        name="out_proj",
    )(ya, yb, w_stack, w_stack, x, *g_arg)


def _ffn_up_body(xg_ref, ss_ref, wg_ref, wu_ref, o_ref, wg_bf, wu_bf, *, d_model):
    @pl.when(pl.program_id(1) == 0)
    def _():
        wg_bf[...] = wg_ref[...].astype(BF16)
        wu_bf[...] = wu_ref[...].astype(BF16)

    h = xg_ref[...]
    rs = _row_scale(ss_ref, d_model, o_ref.shape[1])
    g = rs * jnp.dot(h, wg_bf[...], preferred_element_type=F32)
    u = rs * jnp.dot(h, wu_bf[...], preferred_element_type=F32)
    o_ref[...] = ((g * _sigmoid(g)) * u).astype(o_ref.dtype)


def ffn_up(xg, ss, wg_stack, wu_stack, layer):
    tm, tn = TILE_UP
    T, K = xg.shape
    F = wg_stack.shape[-1]
    parts = ss.shape[0]
    return pl.pallas_call(
        functools.partial(_ffn_up_body, d_model=K),
        grid=(F // tn, T // tm),
        in_specs=[
            pl.BlockSpec((tm, K), lambda n, m: (m, 0)),
            pl.BlockSpec((parts, tm, V7X_LANES), lambda n, m: (0, m, 0)),
            pl.BlockSpec((None, K, tn), lambda n, m: (layer, 0, n)),
            pl.BlockSpec((None, K, tn), lambda n, m: (layer, 0, n)),
        ],
        out_specs=pl.BlockSpec((tm, tn), lambda n, m: (m, n)),
        out_shape=jax.ShapeDtypeStruct((T, F), BF16),
        scratch_shapes=[pltpu.VMEM((K, tn), BF16), pltpu.VMEM((K, tn), BF16)],
        compiler_params=_params("arbitrary", "arbitrary"),
        name="ffn_up",
    )(xg, ss, wg_stack, wu_stack)


def _ffn_down_body(*refs, emit_norm):
    if emit_norm:
        a_ref, w_ref, x_ref, gn_ref, o_ref, xg_ref, ss_ref, wbf_ref = refs
    else:
        a_ref, w_ref, x_ref, o_ref, wbf_ref = refs
        gn_ref = xg_ref = ss_ref = None

    @pl.when(pl.program_id(1) == 0)
    def _():
        wbf_ref[...] = w_ref[...].astype(BF16)

    acc = jnp.dot(a_ref[...], wbf_ref[...], preferred_element_type=F32)
    _emit_residual(x_ref[...] + acc, gn_ref, o_ref, xg_ref, ss_ref)


def ffn_down(a, w_stack, layer, x, next_gain, next_layer):
    tm, tn = TILE_DOWN
    T, F = a.shape
    N = w_stack.shape[-1]
    g_in, g_arg, outs, shapes = _residual_specs(T, N, tm, tn, next_gain, next_layer)
    return pl.pallas_call(
        functools.partial(_ffn_down_body, emit_norm=next_gain is not None),
        grid=(N // tn, T // tm),
        in_specs=[
            pl.BlockSpec((tm, F), lambda n, m: (m, 0)),
            pl.BlockSpec((None, F, tn), lambda n, m: (layer, 0, n)),
            pl.BlockSpec((tm, tn), lambda n, m: (m, n)),
        ] + g_in,
        out_specs=outs,
        out_shape=shapes,
        scratch_shapes=[pltpu.VMEM((F, tn), BF16)],
        compiler_params=_params("arbitrary", "arbitrary"),
        name="ffn_down",
    )(a, w_stack, x, *g_arg)


def _lru_body(xa_ref, gate_ref, cw_ref, cb_ref, wg_ref, ba_ref, bx_ref, lam_ref, y_ref,
              xe_ref, hc_ref, *, ts):
    i = pl.program_id(1)
    pad = V7X_SUBLANES
    width = xa_ref.shape[1]

    @pl.when(i == 0)
    def _():
        xe_ref[0:pad, :] = jnp.zeros((pad, width), F32)
        hc_ref[...] = jnp.zeros(hc_ref.shape, F32)

    xe_ref[pad:pad + ts, :] = xa_ref[...]
    xc = cw_ref[0:1, :] * xe_ref[pl.ds(pad - (LRU_CONV - 1), ts), :]
    for k in range(1, LRU_CONV):
        xc = xc + cw_ref[k:k + 1, :] * xe_ref[pl.ds(pad - (LRU_CONV - 1) + k, ts), :]
    xc = xc + cb_ref[...]
    xe_ref[0:pad, :] = xe_ref[ts:ts + pad, :]

    xcb = xc.astype(BF16)
    r_parts, i_parts = [], []
    for h in range(LRU_HEADS):
        gh = jnp.dot(xcb[:, h * HEAD_DIM:(h + 1) * HEAD_DIM], wg_ref[h].astype(BF16),
                     preferred_element_type=F32)
        r_parts.append(gh[:, :HEAD_DIM])
        i_parts.append(gh[:, HEAD_DIM:])
    r = _sigmoid(jnp.concatenate(r_parts, axis=1) + ba_ref[...])
    ig = _sigmoid(jnp.concatenate(i_parts, axis=1) + bx_ref[...])

    nl = -lam_ref[...]
    softplus = jnp.maximum(nl, 0.0) + jnp.log1p(jnp.exp(-jnp.abs(nl)))
    log_a = (-LRU_C * r) * softplus
    a = jnp.exp(log_a)
    mult = jnp.sqrt(jnp.maximum(jnp.tanh(-log_a) * (a * a + 1.0), 0.0))
    row = lax.broadcasted_iota(jnp.int32, (pad, width), 0)
    first = jnp.where(jnp.logical_and(i == 0, row == 0), 1.0, mult[:pad])
    mult = jnp.concatenate([first, mult[pad:]], axis=0)
    u = (mult * ig) * xc

    tiles = ts // pad
    a3 = a.reshape(tiles, pad, width)
    u3 = u.reshape(tiles, pad, width)
    subrow = lax.broadcasted_iota(jnp.int32, (tiles, pad, width), 1)
    d = 1
    while d < pad:
        keep = subrow >= d
        a_s = jnp.where(keep, pltpu.roll(a3, d, 1), 1.0)
        u_s = jnp.where(keep, pltpu.roll(u3, d, 1), 0.0)
        u3 = a3 * u_s + u3
        a3 = a3 * a_s
        d *= 2
    h_prev = hc_ref[0:1, :]
    h_tiles = []
    for t in range(tiles):
        h_tile = a3[t] * h_prev + u3[t]
        h_tiles.append(h_tile)
        h_prev = h_tile[pad - 1:pad, :]
    hc_ref[0:1, :] = h_prev
    hseq = jnp.concatenate(h_tiles, axis=0)

    g = gate_ref[...]
    c = 0.7978845608028654
    cdf = 0.5 * (1.0 + jnp.tanh(c * (g + 0.044715 * (g * g * g))))
    y_ref[...] = (hseq * (g * cdf)).astype(y_ref.dtype)


def lru_mixer(proj_out, batch, conv_w, conv_b, w_gates, ba, bx, lam, layer, ts=256):
    T = proj_out.shape[0]
    W = LRU_HEADS * HEAD_DIM
    nt = T // batch // ts
    vec = pl.BlockSpec((None, 1, W), lambda b, i: (layer, 0, 0))
    return pl.pallas_call(
        functools.partial(_lru_body, ts=ts),
        grid=(batch, nt),
        in_specs=[
            pl.BlockSpec((ts, W), lambda b, i: (b * nt + i, 0)),
            pl.BlockSpec((ts, W), lambda b, i: (b * nt + i, 1)),
            pl.BlockSpec((None, LRU_CONV, W), lambda b, i: (layer, 0, 0)),
            vec,
            pl.BlockSpec((None, LRU_HEADS, HEAD_DIM, 2 * HEAD_DIM), lambda b, i: (layer, 0, 0, 0)),
            vec, vec, vec,
        ],
        out_specs=pl.BlockSpec((ts, W), lambda b, i: (b * nt + i, 0)),
        out_shape=jax.ShapeDtypeStruct((T, W), BF16),
        scratch_shapes=[
            pltpu.VMEM((ts + 8, W), F32),
            pltpu.VMEM((8, W), F32),
        ],
        compiler_params=_params("arbitrary", "arbitrary"),
        name="rg_lru",
    )(proj_out, proj_out, conv_w, conv_b, w_gates, ba, bx, lam)


def _cumsum_rows(x):
    n = x.shape[0]
    row = lax.broadcasted_iota(jnp.int32, x.shape, 0)
    d = 1
    while d < n:
        x = x + jnp.where(row >= d, pltpu.roll(x, d, 0), 0.0)
        d *= 2
    return x


def _hgrn_body(q_ref, f_ref, v_ref, g_ref, lbl_ref, ng_ref, y_ref, st_ref, rows_ref, *, ts, layer):
    @pl.when(pl.program_id(0) == 0)
    def _():
        st_ref[...] = jnp.zeros(st_ref.shape, F32)

    logits = lbl_ref[...]
    e = jnp.exp(logits - jnp.max(logits, axis=0, keepdims=True))
    sm = e / jnp.sum(e, axis=0, keepdims=True)
    lb = jnp.sum(sm[0:layer + 1, :], axis=0, keepdims=True) - sm[0:1, :]
    one_m_lb = 1.0 - lb
    ng = ng_ref[...]
    n = HGRN_SUB
    sub = V7X_SUBLANES
    batch, _, width = q_ref.shape
    subrow = lax.broadcasted_iota(jnp.int32, (sub, width), 0)
    nt_dims = (((1,), (1,)), ((), ()))
    tn_dims = (((0,), (0,)), ((), ()))

    t_idx = lax.broadcasted_iota(jnp.int32, (n, n), 0)
    s_idx = lax.broadcasted_iota(jnp.int32, (n, n), 1)
    diag_mask = t_idx == s_idx
    level_mask = {}
    h = 1
    while h < n:
        level_mask[h] = jnp.logical_and(t_idx // (2 * h) == s_idx // (2 * h),
                                        jnp.logical_and((t_idx // h) % 2 == 1, (s_idx // h) % 2 == 0))
        h *= 2

    def ref_rows(b, h):
        pieces = []
        for i in range(n // sub):
            base = i * sub
            if 2 * h >= sub:
                rho = (base // (2 * h)) * (2 * h) + h - 1
                pieces.append(jnp.broadcast_to(rows_ref[b, pl.ds(rho, 1), :], (sub, width)))
            else:
                piece = None
                for c in range(sub // (2 * h)):
                    rho = base + c * 2 * h + h - 1
                    rowv = jnp.broadcast_to(rows_ref[b, pl.ds(rho, 1), :], (sub, width))
                    piece = rowv if piece is None else jnp.where(subrow >= c * 2 * h, rowv, piece)
                pieces.append(piece)
        return jnp.concatenate(pieces, axis=0)

    def sub_block(b, j):
        r0 = pl.multiple_of(j * n, n)
        f = f_ref[b, pl.ds(r0, n), :]
        q = q_ref[b, pl.ds(r0, n), :]
        v = v_ref[b, pl.ds(r0, n), :]
        sig = _sigmoid(f)
        fg = jnp.maximum(lb + one_m_lb * sig, F_FLOOR)
        k = one_m_lb * (1.0 - sig)
        qf = q * _sigmoid(q)
        bc = _cumsum_rows(jnp.log(fg)) * LOG2_E
        rows_ref[b] = bc
        b_tot = bc[n - 1:n, :]
        qd = (qf * jnp.exp2(bc)).astype(BF16)
        kd = (k * jnp.exp2(b_tot - bc)).astype(BF16)
        e_tot = jnp.exp2(b_tot)
        vb = v.astype(BF16)
        kb = k.astype(BF16)
        q01 = jnp.concatenate([qf.astype(BF16), (qf * fg).astype(BF16)], axis=0)
        levels = []
        h = n // 2
        while h >= 2:
            br = ref_rows(b, h)
            ql = (qf * jnp.exp2(jnp.minimum(bc - br, 0.0))).astype(BF16)
            kl = (k * jnp.exp2(jnp.minimum(br - bc, 0.0))).astype(BF16)
            levels.append((h, ql, kl))
            h //= 2

        heads = [slice(hd * HEAD_DIM, (hd + 1) * HEAD_DIM) for hd in range(HGRN_HEADS)]
        ws, inters = [], []
        for hd, sl in enumerate(heads):
            r01 = lax.dot_general(q01[:, sl], kb[:, sl], nt_dims, preferred_element_type=F32)
            w = jnp.where(diag_mask, r01[:n], 0.0) + jnp.where(level_mask[1], r01[n:], 0.0)
            for h, ql, kl in levels:
                r = lax.dot_general(ql[:, sl], kl[:, sl], nt_dims, preferred_element_type=F32)
                w = w + jnp.where(level_mask[h], r, 0.0)
            ws.append(w.astype(BF16))
            st = st_ref[b, hd]
            inters.append(lax.dot_general(qd[:, sl], st.astype(BF16), nt_dims,
                                          preferred_element_type=F32))
            d_st = lax.dot_general(vb[:, sl], kd[:, sl], tn_dims, preferred_element_type=F32)
            st_ref[b, hd] = st * e_tot[:, sl] + d_st
        outs = [inters[hd] + jnp.dot(ws[hd], vb[:, sl], preferred_element_type=F32)
                for hd, sl in enumerate(heads)]
        return jnp.concatenate(outs, axis=1)

    def finish(b, j, o_all):
        r0 = pl.multiple_of(j * n, n)
        g = g_ref[b, pl.ds(r0, n), :]
        outs = []
        for hd in range(HGRN_HEADS):
            o = o_all[:, hd * HEAD_DIM:(hd + 1) * HEAD_DIM]
            outs.append(o * lax.rsqrt(jnp.mean(o * o, axis=-1, keepdims=True) + EPS))
        o_n = jnp.concatenate(outs, axis=1) * ng
        y_ref[b, pl.ds(r0, n), :] = (o_n * (g * _sigmoid(g))).astype(y_ref.dtype)

    def step(j, o_prev):
        for b in range(batch):
            finish(b, j - 1, o_prev[b])
        return tuple(sub_block(b, j) for b in range(batch))

    n_sub = ts // n
    o_last = lax.fori_loop(1, n_sub, step, tuple(sub_block(b, 0) for b in range(batch)))
    for b in range(batch):
        finish(b, n_sub - 1, o_last[b])


def hgrn_mixer(proj_out, batch, lb_logits, norm_g, layer, ts=256):
    T = proj_out.shape[0]
    W = HGRN_HEADS * HEAD_DIM
    seq = T // batch
    n_layers = lb_logits.shape[0]
    p3 = proj_out.reshape(batch, seq, proj_out.shape[1])

    def col(section):
        return pl.BlockSpec((batch, ts, W), lambda i: (0, i, section))

    y = pl.pallas_call(
        functools.partial(_hgrn_body, ts=ts, layer=layer),
        grid=(seq // ts,),
        in_specs=[
            col(2), col(3), col(4), col(5),
            pl.BlockSpec((n_layers, W), lambda i: (0, 0)),
            pl.BlockSpec((None, 1, W), lambda i: (layer, 0, 0)),
        ],
        out_specs=pl.BlockSpec((batch, ts, W), lambda i: (0, i, 0)),
        out_shape=jax.ShapeDtypeStruct((batch, seq, W), BF16),
        scratch_shapes=[
            pltpu.VMEM((batch, HGRN_HEADS, HEAD_DIM, HEAD_DIM), F32),
            pltpu.VMEM((batch, HGRN_SUB, W), F32),
        ],
        compiler_params=_params("arbitrary"),
        name="hgrn2",
    )(p3, p3, p3, p3, lb_logits, norm_g)
    return y.reshape(T, W)


def _odd_body(sb_ref, sc_ref, sv_ref, cu_ref, cg_ref, scw_ref, cfw_ref, cfb_ref, lng_ref, lnb_ref,
              yc_ref, yd_ref, pe_ref, ge_ref, d_ref, *, ts, rows, conv_rows, conv_lanes):
    i = pl.program_id(1)
    width = sb_ref.shape[1]
    sub = V7X_SUBLANES
    pad_p = sub
    pad_g = 4 * sub

    @pl.when(i == 0)
    def _():
        pe_ref[0:pad_p, :] = jnp.zeros((pad_p, width), F32)
        ge_ref[0:pad_g, :] = jnp.zeros((pad_g, width), F32)

    pe_ref[pad_p:pad_p + ts, :] = sc_ref[...] * sv_ref[...]
    ge_ref[pad_g:pad_g + ts, :] = cu_ref[...] * _sigmoid(cg_ref[...])

    phases = {}
    for k in range(CF_CONV):
        off = pad_g - (CF_CONV - 1) + k
        phases.setdefault(off % sub, []).append((k, off - off % sub))
    for r0 in range(0, ts, conv_rows):
        for c0 in range(0, width, conv_lanes):
            cols = slice(c0, c0 + conv_lanes)
            acc = None
            for p, taps in sorted(phases.items()):
                nrows = conv_rows if p == 0 else conv_rows + sub
                z = None
                for k, q in taps:
                    term = cfw_ref[k:k + 1, cols] * ge_ref[r0 + q:r0 + q + nrows, cols]
                    z = term if z is None else z + term
                part = z if p == 0 else z[p:p + conv_rows]
                acc = part if acc is None else acc + part
            d_ref[r0:r0 + conv_rows, cols] = acc + cfb_ref[:, cols]

    for r0 in range(0, ts, rows):
        base = pad_p - (SC_CONV - 1) + r0
        acc = scw_ref[0:1, :] * pe_ref[base:base + rows, :]
        for k in range(1, SC_CONV):
            acc = acc + scw_ref[k:k + 1, :] * pe_ref[base + k:base + k + rows, :]
        yc_ref[r0:r0 + rows, :] = (sb_ref[r0:r0 + rows, :] * acc).astype(yc_ref.dtype)

        d = d_ref[r0:r0 + rows, :]
        mu = jnp.mean(d, axis=-1, keepdims=True)
        dc = d - mu
        var = jnp.mean(dc * dc, axis=-1, keepdims=True)
        z = (dc * lax.rsqrt(var + EPS)) * lng_ref[...] + lnb_ref[...]
        yd_ref[r0:r0 + rows, :] = (z * _sigmoid(z)).astype(yd_ref.dtype)

    pe_ref[0:pad_p, :] = pe_ref[ts:ts + pad_p, :]
    ge_ref[0:pad_g, :] = ge_ref[ts:ts + pad_g, :]


def odd_mixer(proj_out, batch, sc_w, cf_w, cf_b, ln_g, ln_b, layer, ts=256, rows=32,
              conv_rows=64, conv_lanes=256):
    T = proj_out.shape[0]
    W = proj_out.shape[1] // 5
    nt = T // batch // ts

    def col(section):
        return pl.BlockSpec((ts, W), lambda b, i: (b * nt + i, section))

    vec = pl.BlockSpec((None, 1, W), lambda b, i: (layer, 0, 0))
    out = pl.BlockSpec((ts, W), lambda b, i: (b * nt + i, 0))
    return pl.pallas_call(
        functools.partial(_odd_body, ts=ts, rows=rows, conv_rows=conv_rows, conv_lanes=conv_lanes),
        grid=(batch, nt),
        in_specs=[
            col(0), col(1), col(2), col(3), col(4),
            pl.BlockSpec((None, SC_CONV, W), lambda b, i: (layer, 0, 0)),
            pl.BlockSpec((None, CF_CONV, W), lambda b, i: (layer, 0, 0)),
            vec, vec, vec,
        ],
        out_specs=[out, out],
        out_shape=[jax.ShapeDtypeStruct((T, W), BF16), jax.ShapeDtypeStruct((T, W), BF16)],
        scratch_shapes=[
            pltpu.VMEM((ts + 8, W), F32),
            pltpu.VMEM((ts + 32, W), F32),
            pltpu.VMEM((ts, W), F32),
        ],
        compiler_params=_params("arbitrary", "arbitrary"),
        name="odd_mixer",
    )(proj_out, proj_out, proj_out, proj_out, proj_out, sc_w, cf_w, cf_b, ln_g, ln_b)


def kernel(x, ln_mix_g, ln_ffn_g, ln_final_g, ev_w_in, ev_b_in, lru_conv_w, lru_conv_b, lru_wa, lru_ba, lru_wx, lru_bx, lru_lambda, hgrn_lb_logits, hgrn_norm_g, ev_w_out, od_w_in, od_b_in, sc_conv_w, cf_conv_w, cf_conv_b, cf_ln_g, cf_ln_b, od_w_out, ffn_w_gate, ffn_w_up, ffn_w_down):
    B, S, D = x.shape
    depth = ln_mix_g.shape[0]
    xt = x.reshape(B * S, D)

    def rows(p):
        return p.reshape(p.shape[0], 1, p.shape[-1])

    ln_mix = rows(ln_mix_g)
    ln_ffn = rows(ln_ffn_g)
    ln_fin = ln_final_g.reshape(1, 1, D)
    ev_b = rows(ev_b_in)
    od_b = rows(od_b_in)
    w_gates = jnp.concatenate([lru_wa, lru_wx], axis=-1)
    n_even = lru_wa.shape[0]
    lru_ba2 = lru_ba.reshape(n_even, 1, -1)
    lru_bx2 = lru_bx.reshape(n_even, 1, -1)

    xg, ss = prenorm(xt, ln_mix, 0)
    for layer in range(depth):
        j = layer // 2
        if layer % 2 == 0:
            p = proj(xg, ss, ev_w_in, ev_b, j)
            ya = lru_mixer(p, B, lru_conv_w, rows(lru_conv_b), w_gates, lru_ba2, lru_bx2,
                           rows(lru_lambda), j)
            yb = hgrn_mixer(p, B, hgrn_lb_logits, rows(hgrn_norm_g), j)
            xt, xg, ss = out_proj(ya, yb, ev_w_out, j, xt, ln_ffn, layer)
        else:
            p = proj(xg, ss, od_w_in, od_b, j)
            yc, yd = odd_mixer(p, B, sc_conv_w, cf_conv_w, rows(cf_conv_b), rows(cf_ln_g),
                               rows(cf_ln_b), j)
            xt, xg, ss = out_proj(yc, yd, od_w_out, j, xt, ln_ffn, layer)
        a = ffn_up(xg, ss, ffn_w_gate, ffn_w_up, layer)
        if layer + 1 < depth:
            xt, xg, ss = ffn_down(a, ffn_w_down, layer, xt, ln_mix, layer + 1)
        else:
            xt = ffn_down(a, ffn_w_down, layer, xt, None, 0)
    out = rmsnorm(xt, ln_fin, 0, F32)
    return out.reshape(B, S, D)
```

```python
import functools

import jax
import jax.numpy as jnp
from jax import lax
from jax.experimental import pallas as pl
from jax.experimental.pallas import tpu as pltpu

F32 = jnp.float32
BF16 = jnp.bfloat16

EPS = 1e-6
F_FLOOR = 1e-30
LRU_C = 8.0
LRU_HEADS = 8
LRU_CONV = 4
HGRN_HEADS = 8
HEAD_DIM = 128
SC_CONV = 3
CF_CONV = 31
HGRN_SUB = 32
LOG2_E = 1.4426950408889634

V7X_LANES = 128
V7X_SUBLANES = 8
V7X_VMEM_BYTES = 64 * 2**20
MIB = 2**20
KERNEL_VMEM_LIMIT = V7X_VMEM_BYTES - 8 * MIB

TILE_IN = (1024, 1024)
TILE_OUT = (1024, 1024)
TILE_UP = (1024, 512)
TILE_DOWN = (1024, 512)


def _params(*semantics):
    return pltpu.CompilerParams(dimension_semantics=semantics, vmem_limit_bytes=KERNEL_VMEM_LIMIT)


def _sigmoid(x):
    return jax.nn.sigmoid(x)


def _lane_tile(x, width):
    return jnp.concatenate([x] * (width // V7X_LANES), axis=1)


def _sumsq_lanes(x):
    return jnp.broadcast_to(jnp.sum(x * x, axis=-1, keepdims=True), (x.shape[0], V7X_LANES))


def _row_scale(ss_ref, d_model, width):
    ms = jnp.sum(ss_ref[...], axis=0) * (1.0 / d_model)
    return _lane_tile(lax.rsqrt(ms + EPS), width)


def _prenorm_body(x_ref, g_ref, xg_ref, ss_ref):
    x = x_ref[...]
    xg_ref[...] = (x * g_ref[...]).astype(xg_ref.dtype)
    ss_ref[...] = _sumsq_lanes(x)


def prenorm(x, g_stack, layer, tm=512):
    T, D = x.shape
    return pl.pallas_call(
        _prenorm_body,
        grid=(T // tm,),
        in_specs=[
            pl.BlockSpec((tm, D), lambda i: (i, 0)),
            pl.BlockSpec((None, 1, D), lambda i: (layer, 0, 0)),
        ],
        out_specs=[
            pl.BlockSpec((tm, D), lambda i: (i, 0)),
            pl.BlockSpec((None, tm, V7X_LANES), lambda i: (0, i, 0)),
        ],
        out_shape=[jax.ShapeDtypeStruct((T, D), BF16),
                   jax.ShapeDtypeStruct((1, T, V7X_LANES), F32)],
        compiler_params=_params("arbitrary"),
        name="prenorm",
    )(x, g_stack)


def _rmsnorm_body(x_ref, g_ref, o_ref):
    x = x_ref[...]
    ms = jnp.mean(x * x, axis=-1, keepdims=True)
    o_ref[...] = ((x * lax.rsqrt(ms + EPS)) * g_ref[...]).astype(o_ref.dtype)


def rmsnorm(x, g_stack, layer, out_dtype, tm=512):
    T, D = x.shape
    return pl.pallas_call(
        _rmsnorm_body,
        grid=(T // tm,),
        in_specs=[
            pl.BlockSpec((tm, D), lambda i: (i, 0)),
            pl.BlockSpec((None, 1, D), lambda i: (layer, 0, 0)),
        ],
        out_specs=pl.BlockSpec((tm, D), lambda i: (i, 0)),
        out_shape=jax.ShapeDtypeStruct((T, D), out_dtype),
        compiler_params=_params("arbitrary"),
        name="rmsnorm",
    )(x, g_stack)


def _proj_body(xg_ref, ss_ref, w_ref, b_ref, o_ref, wbf_ref, *, d_model):
    @pl.when(pl.program_id(1) == 0)
    def _():
        wbf_ref[...] = w_ref[...].astype(BF16)

    acc = jnp.dot(xg_ref[...], wbf_ref[...], preferred_element_type=F32)
    o_ref[...] = _row_scale(ss_ref, d_model, acc.shape[1]) * acc + b_ref[...]


def proj(xg, ss, w_stack, b_stack, layer):
    tm, tn = TILE_IN
    T, K = xg.shape
    N = w_stack.shape[-1]
    parts = ss.shape[0]
    return pl.pallas_call(
        functools.partial(_proj_body, d_model=K),
        grid=(N // tn, T // tm),
        in_specs=[
            pl.BlockSpec((tm, K), lambda n, m: (m, 0)),
            pl.BlockSpec((parts, tm, V7X_LANES), lambda n, m: (0, m, 0)),
            pl.BlockSpec((None, K, tn), lambda n, m: (layer, 0, n)),
            pl.BlockSpec((None, 1, tn), lambda n, m: (layer, 0, n)),
        ],
        out_specs=pl.BlockSpec((tm, tn), lambda n, m: (m, n)),
        out_shape=jax.ShapeDtypeStruct((T, N), F32),
        scratch_shapes=[pltpu.VMEM((K, tn), BF16)],
        compiler_params=_params("arbitrary", "arbitrary"),
        name="in_proj",
    )(xg, ss, w_stack, b_stack)


def _emit_residual(xn, gn_ref, o_ref, xg_ref, ss_ref):
    o_ref[...] = xn
    if gn_ref is not None:
        xg_ref[...] = (xn * gn_ref[...]).astype(xg_ref.dtype)
        ss_ref[...] = _sumsq_lanes(xn)


def _out_body(*refs, emit_norm):
    if emit_norm:
        ya_ref, yb_ref, wa_ref, wb_ref, x_ref, gn_ref, o_ref, xg_ref, ss_ref, wa_bf, wb_bf = refs
    else:
        ya_ref, yb_ref, wa_ref, wb_ref, x_ref, o_ref, wa_bf, wb_bf = refs
        gn_ref = xg_ref = ss_ref = None

    @pl.when(pl.program_id(1) == 0)
    def _():
        wa_bf[...] = wa_ref[...].astype(BF16)
        wb_bf[...] = wb_ref[...].astype(BF16)

    acc = jnp.dot(ya_ref[...], wa_bf[...], preferred_element_type=F32)
    acc = acc + jnp.dot(yb_ref[...], wb_bf[...], preferred_element_type=F32)
    _emit_residual(x_ref[...] + acc, gn_ref, o_ref, xg_ref, ss_ref)


def _residual_specs(T, N, tm, tn, next_gain, next_layer):
    x_spec = pl.BlockSpec((tm, tn), lambda n, m: (m, n))
    if next_gain is None:
        return [], [], x_spec, jax.ShapeDtypeStruct((T, N), F32)
    ins = [pl.BlockSpec((None, 1, tn), lambda n, m: (next_layer, 0, n))]
    outs = [x_spec, x_spec, pl.BlockSpec((None, tm, V7X_LANES), lambda n, m: (n, m, 0))]
    shapes = [jax.ShapeDtypeStruct((T, N), F32), jax.ShapeDtypeStruct((T, N), BF16),
              jax.ShapeDtypeStruct((N // tn, T, V7X_LANES), F32)]
    return ins, [next_gain], outs, shapes


def out_proj(ya, yb, w_stack, layer, x, next_gain, next_layer):
    tm, tn = TILE_OUT
    T, Kh = ya.shape
    N = w_stack.shape[-1]
    g_in, g_arg, outs, shapes = _residual_specs(T, N, tm, tn, next_gain, next_layer)
    return pl.pallas_call(
        functools.partial(_out_body, emit_norm=next_gain is not None),
        grid=(N // tn, T // tm),
        in_specs=[
            pl.BlockSpec((tm, Kh), lambda n, m: (m, 0)),
            pl.BlockSpec((tm, Kh), lambda n, m: (m, 0)),
            pl.BlockSpec((None, Kh, tn), lambda n, m: (layer, 0, n)),
            pl.BlockSpec((None, Kh, tn), lambda n, m: (layer, 1, n)),
            pl.BlockSpec((tm, tn), lambda n, m: (m, n)),
        ] + g_in,
        out_specs=outs,
        out_shape=shapes,
        scratch_shapes=[pltpu.VMEM((Kh, tn), BF16), pltpu.VMEM((Kh, tn), BF16)],
        compiler_params=_params("arbitrary", "arbitrary"),
        name="out_proj",
    )(ya, yb, w_stack, w_stack, x, *g_arg)


def _ffn_up_body(xg_ref, ss_ref, wg_ref, wu_ref, o_ref, wg_bf, wu_bf, *, d_model):
    @pl.when(pl.program_id(1) == 0)
    def _():
        wg_bf[...] = wg_ref[...].astype(BF16)
        wu_bf[...] = wu_ref[...].astype(BF16)

    h = xg_ref[...]
    rs = _row_scale(ss_ref, d_model, o_ref.shape[1])
    g = rs * jnp.dot(h, wg_bf[...], preferred_element_type=F32)
    u = rs * jnp.dot(h, wu_bf[...], preferred_element_type=F32)
    o_ref[...] = ((g * _sigmoid(g)) * u).astype(o_ref.dtype)


def ffn_up(xg, ss, wg_stack, wu_stack, layer):
    tm, tn = TILE_UP
    T, K = xg.shape
    F = wg_stack.shape[-1]
    parts = ss.shape[0]
    return pl.pallas_call(
        functools.partial(_ffn_up_body, d_model=K),
        grid=(F // tn, T // tm),
        in_specs=[
            pl.BlockSpec((tm, K), lambda n, m: (m, 0)),
            pl.BlockSpec((parts, tm, V7X_LANES), lambda n, m: (0, m, 0)),
            pl.BlockSpec((None, K, tn), lambda n, m: (layer, 0, n)),
            pl.BlockSpec((None, K, tn), lambda n, m: (layer, 0, n)),
        ],
        out_specs=pl.BlockSpec((tm, tn), lambda n, m: (m, n)),
        out_shape=jax.ShapeDtypeStruct((T, F), BF16),
        scratch_shapes=[pltpu.VMEM((K, tn), BF16), pltpu.VMEM((K, tn), BF16)],
        compiler_params=_params("arbitrary", "arbitrary"),
        name="ffn_up",
    )(xg, ss, wg_stack, wu_stack)


def _ffn_down_body(*refs, emit_norm):
    if emit_norm:
        a_ref, w_ref, x_ref, gn_ref, o_ref, xg_ref, ss_ref, wbf_ref = refs
    else:
        a_ref, w_ref, x_ref, o_ref, wbf_ref = refs
        gn_ref = xg_ref = ss_ref = None

    @pl.when(pl.program_id(1) == 0)
    def _():
        wbf_ref[...] = w_ref[...].astype(BF16)

    acc = jnp.dot(a_ref[...], wbf_ref[...], preferred_element_type=F32)
    _emit_residual(x_ref[...] + acc, gn_ref, o_ref, xg_ref, ss_ref)


def ffn_down(a, w_stack, layer, x, next_gain, next_layer):
    tm, tn = TILE_DOWN
    T, F = a.shape
    N = w_stack.shape[-1]
    g_in, g_arg, outs, shapes = _residual_specs(T, N, tm, tn, next_gain, next_layer)
    return pl.pallas_call(
        functools.partial(_ffn_down_body, emit_norm=next_gain is not None),
        grid=(N // tn, T // tm),
        in_specs=[
            pl.BlockSpec((tm, F), lambda n, m: (m, 0)),
            pl.BlockSpec((None, F, tn), lambda n, m: (layer, 0, n), pipeline_mode=pl.Buffered(1)),
            pl.BlockSpec((tm, tn), lambda n, m: (m, n)),
        ] + g_in,
        out_specs=outs,
        out_shape=shapes,
        scratch_shapes=[pltpu.VMEM((F, tn), BF16)],
        compiler_params=_params("arbitrary", "arbitrary"),
        name="ffn_down",
    )(a, w_stack, x, *g_arg)


def _lru_body(xa_ref, gate_ref, cw_ref, cb_ref, wg_ref, ba_ref, bx_ref, lam_ref, y_ref,
              xe_ref, hc_ref, *, ts):
    i = pl.program_id(1)
    pad = V7X_SUBLANES
    width = xa_ref.shape[1]

    @pl.when(i == 0)
    def _():
        xe_ref[0:pad, :] = jnp.zeros((pad, width), F32)
        hc_ref[...] = jnp.zeros(hc_ref.shape, F32)

    xe_ref[pad:pad + ts, :] = xa_ref[...]
    xc = cw_ref[0:1, :] * xe_ref[pl.ds(pad - (LRU_CONV - 1), ts), :]
    for k in range(1, LRU_CONV):
        xc = xc + cw_ref[k:k + 1, :] * xe_ref[pl.ds(pad - (LRU_CONV - 1) + k, ts), :]
    xc = xc + cb_ref[...]
    xe_ref[0:pad, :] = xe_ref[ts:ts + pad, :]

    xcb = xc.astype(BF16)
    r_parts, i_parts = [], []
    for h in range(LRU_HEADS):
        gh = jnp.dot(xcb[:, h * HEAD_DIM:(h + 1) * HEAD_DIM], wg_ref[h].astype(BF16),
                     preferred_element_type=F32)
        r_parts.append(gh[:, :HEAD_DIM])
        i_parts.append(gh[:, HEAD_DIM:])
    r = _sigmoid(jnp.concatenate(r_parts, axis=1) + ba_ref[...])
    ig = _sigmoid(jnp.concatenate(i_parts, axis=1) + bx_ref[...])

    nl = -lam_ref[...]
    softplus = jnp.maximum(nl, 0.0) + jnp.log1p(jnp.exp(-jnp.abs(nl)))
    log_a = (-LRU_C * r) * softplus
    a = jnp.exp(log_a)
    mult = jnp.sqrt(jnp.maximum(jnp.tanh(-log_a) * (a * a + 1.0), 0.0))
    row = lax.broadcasted_iota(jnp.int32, (pad, width), 0)
    first = jnp.where(jnp.logical_and(i == 0, row == 0), 1.0, mult[:pad])
    mult = jnp.concatenate([first, mult[pad:]], axis=0)
    u = (mult * ig) * xc

    tiles = ts // pad
    a3 = a.reshape(tiles, pad, width)
    u3 = u.reshape(tiles, pad, width)
    subrow = lax.broadcasted_iota(jnp.int32, (tiles, pad, width), 1)
    d = 1
    while d < pad:
        keep = subrow >= d
        a_s = jnp.where(keep, pltpu.roll(a3, d, 1), 1.0)
        u_s = jnp.where(keep, pltpu.roll(u3, d, 1), 0.0)
        u3 = a3 * u_s + u3
        a3 = a3 * a_s
        d *= 2
    h_prev = hc_ref[0:1, :]
    h_tiles = []
    for t in range(tiles):
        h_tile = a3[t] * h_prev + u3[t]
        h_tiles.append(h_tile)
        h_prev = h_tile[pad - 1:pad, :]
    hc_ref[0:1, :] = h_prev
    hseq = jnp.concatenate(h_tiles, axis=0)

    g = gate_ref[...]
    c = 0.7978845608028654
    cdf = 0.5 * (1.0 + jnp.tanh(c * (g + 0.044715 * (g * g * g))))
    y_ref[...] = (hseq * (g * cdf)).astype(y_ref.dtype)


def lru_mixer(proj_out, batch, conv_w, conv_b, w_gates, ba, bx, lam, layer, ts=256):
    T = proj_out.shape[0]
    W = LRU_HEADS * HEAD_DIM
    nt = T // batch // ts
    vec = pl.BlockSpec((None, 1, W), lambda b, i: (layer, 0, 0))
    return pl.pallas_call(
        functools.partial(_lru_body, ts=ts),
        grid=(batch, nt),
        in_specs=[
            pl.BlockSpec((ts, W), lambda b, i: (b * nt + i, 0)),
            pl.BlockSpec((ts, W), lambda b, i: (b * nt + i, 1)),
            pl.BlockSpec((None, LRU_CONV, W), lambda b, i: (layer, 0, 0)),
            vec,
            pl.BlockSpec((None, LRU_HEADS, HEAD_DIM, 2 * HEAD_DIM), lambda b, i: (layer, 0, 0, 0)),
            vec, vec, vec,
        ],
        out_specs=pl.BlockSpec((ts, W), lambda b, i: (b * nt + i, 0)),
        out_shape=jax.ShapeDtypeStruct((T, W), BF16),
        scratch_shapes=[
            pltpu.VMEM((ts + 8, W), F32),
            pltpu.VMEM((8, W), F32),
        ],
        compiler_params=_params("arbitrary", "arbitrary"),
        name="rg_lru",
    )(proj_out, proj_out, conv_w, conv_b, w_gates, ba, bx, lam)


def _cumsum_rows(x):
    n = x.shape[0]
    row = lax.broadcasted_iota(jnp.int32, x.shape, 0)
    d = 1
    while d < n:
        x = x + jnp.where(row >= d, pltpu.roll(x, d, 0), 0.0)
        d *= 2
    return x


def _hgrn_body(q_ref, f_ref, v_ref, g_ref, lbl_ref, ng_ref, y_ref, st_ref, rows_ref, *, ts, layer):
    @pl.when(pl.program_id(0) == 0)
    def _():
        st_ref[...] = jnp.zeros(st_ref.shape, F32)

    logits = lbl_ref[...]
    e = jnp.exp(logits - jnp.max(logits, axis=0, keepdims=True))
    sm = e / jnp.sum(e, axis=0, keepdims=True)
    lb = jnp.sum(sm[0:layer + 1, :], axis=0, keepdims=True) - sm[0:1, :]
    one_m_lb = 1.0 - lb
    ng = ng_ref[...]
    n = HGRN_SUB
    sub = V7X_SUBLANES
    batch, _, width = q_ref.shape
    subrow = lax.broadcasted_iota(jnp.int32, (sub, width), 0)
    nt_dims = (((1,), (1,)), ((), ()))
    tn_dims = (((0,), (0,)), ((), ()))

    t_idx = lax.broadcasted_iota(jnp.int32, (n, n), 0)
    s_idx = lax.broadcasted_iota(jnp.int32, (n, n), 1)
    diag_mask = t_idx == s_idx
    level_mask = {}
    h = 1
    while h < n:
        level_mask[h] = jnp.logical_and(t_idx // (2 * h) == s_idx // (2 * h),
                                        jnp.logical_and((t_idx // h) % 2 == 1, (s_idx // h) % 2 == 0))
        h *= 2

    def ref_rows(b, h):
        pieces = []
        for i in range(n // sub):
            base = i * sub
            if 2 * h >= sub:
                rho = (base // (2 * h)) * (2 * h) + h - 1
                pieces.append(jnp.broadcast_to(rows_ref[b, pl.ds(rho, 1), :], (sub, width)))
            else:
                piece = None
                for c in range(sub // (2 * h)):
                    rho = base + c * 2 * h + h - 1
                    rowv = jnp.broadcast_to(rows_ref[b, pl.ds(rho, 1), :], (sub, width))
                    piece = rowv if piece is None else jnp.where(subrow >= c * 2 * h, rowv, piece)
                pieces.append(piece)
        return jnp.concatenate(pieces, axis=0)

    def sub_block(b, j):
        r0 = pl.multiple_of(j * n, n)
        f = f_ref[b, pl.ds(r0, n), :]
        q = q_ref[b, pl.ds(r0, n), :]
        v = v_ref[b, pl.ds(r0, n), :]
        sig = _sigmoid(f)
        fg = jnp.maximum(lb + one_m_lb * sig, F_FLOOR)
        k = one_m_lb * (1.0 - sig)
        qf = q * _sigmoid(q)
        bc = _cumsum_rows(jnp.log(fg)) * LOG2_E
        rows_ref[b] = bc
        b_tot = bc[n - 1:n, :]
        qd = (qf * jnp.exp2(bc)).astype(BF16)
        kd = (k * jnp.exp2(b_tot - bc)).astype(BF16)
        e_tot = jnp.exp2(b_tot)
        vb = v.astype(BF16)
        kb = k.astype(BF16)
        q01 = jnp.concatenate([qf.astype(BF16), (qf * fg).astype(BF16)], axis=0)
        levels = []
        h = n // 2
        while h >= 2:
            br = ref_rows(b, h)
            dist = bc - br
            dec = jnp.exp2(jnp.minimum(dist, -dist))
            levels.append((h, (qf * dec).astype(BF16), (k * dec).astype(BF16)))
            h //= 2

        heads = [slice(hd * HEAD_DIM, (hd + 1) * HEAD_DIM) for hd in range(HGRN_HEADS)]
        ws, inters = [], []
        for hd, sl in enumerate(heads):
            r01 = lax.dot_general(q01[:, sl], kb[:, sl], nt_dims, preferred_element_type=F32)
            w = jnp.where(diag_mask, r01[:n], 0.0)
            w = jnp.where(level_mask[1], r01[n:], w)
            for h, ql, kl in levels:
                r = lax.dot_general(ql[:, sl], kl[:, sl], nt_dims, preferred_element_type=F32)
                w = jnp.where(level_mask[h], r, w)
            ws.append(w.astype(BF16))
            st = st_ref[b, hd]
            inters.append(lax.dot_general(qd[:, sl], st.astype(BF16), nt_dims,
                                          preferred_element_type=F32))
            d_st = lax.dot_general(vb[:, sl], kd[:, sl], tn_dims, preferred_element_type=F32)
            st_ref[b, hd] = st * e_tot[:, sl] + d_st
        outs = [inters[hd] + jnp.dot(ws[hd], vb[:, sl], preferred_element_type=F32)
                for hd, sl in enumerate(heads)]
        return jnp.concatenate(outs, axis=1)

    def finish(b, j, o_all):
        r0 = pl.multiple_of(j * n, n)
        g = g_ref[b, pl.ds(r0, n), :]
        outs = []
        for hd in range(HGRN_HEADS):
            o = o_all[:, hd * HEAD_DIM:(hd + 1) * HEAD_DIM]
            outs.append(o * lax.rsqrt(jnp.mean(o * o, axis=-1, keepdims=True) + EPS))
        o_n = jnp.concatenate(outs, axis=1) * ng
        y_ref[b, pl.ds(r0, n), :] = (o_n * (g * _sigmoid(g))).astype(y_ref.dtype)

    def step(j, o_prev):
        for b in range(batch):
            finish(b, j - 1, o_prev[b])
        return tuple(sub_block(b, j) for b in range(batch))

    n_sub = ts // n
    o_last = lax.fori_loop(1, n_sub, step, tuple(sub_block(b, 0) for b in range(batch)))
    for b in range(batch):
        finish(b, n_sub - 1, o_last[b])


def hgrn_mixer(proj_out, batch, lb_logits, norm_g, layer, ts=256):
    T = proj_out.shape[0]
    W = HGRN_HEADS * HEAD_DIM
    seq = T // batch
    n_layers = lb_logits.shape[0]
    p3 = proj_out.reshape(batch, seq, proj_out.shape[1])

    def col(section):
        return pl.BlockSpec((batch, ts, W), lambda i: (0, i, section))

    y = pl.pallas_call(
        functools.partial(_hgrn_body, ts=ts, layer=layer),
        grid=(seq // ts,),
        in_specs=[
            col(2), col(3), col(4), col(5),
            pl.BlockSpec((n_layers, W), lambda i: (0, 0)),
            pl.BlockSpec((None, 1, W), lambda i: (layer, 0, 0)),
        ],
        out_specs=pl.BlockSpec((batch, ts, W), lambda i: (0, i, 0)),
        out_shape=jax.ShapeDtypeStruct((batch, seq, W), BF16),
        scratch_shapes=[
            pltpu.VMEM((batch, HGRN_HEADS, HEAD_DIM, HEAD_DIM), F32),
            pltpu.VMEM((batch, HGRN_SUB, W), F32),
        ],
        compiler_params=_params("arbitrary"),
        name="hgrn2",
    )(p3, p3, p3, p3, lb_logits, norm_g)
    return y.reshape(T, W)


def _odd_body(sb_ref, sc_ref, sv_ref, cu_ref, cg_ref, scw_ref, cfw_ref, cfb_ref, lng_ref, lnb_ref,
              yc_ref, yd_ref, pe_ref, ge_ref, d_ref, *, ts, rows, conv_rows, conv_lanes):
    i = pl.program_id(1)
    width = sb_ref.shape[1]
    sub = V7X_SUBLANES
    pad_p = sub
    pad_g = 4 * sub

    @pl.when(i == 0)
    def _():
        pe_ref[0:pad_p, :] = jnp.zeros((pad_p, width), F32)
        ge_ref[0:pad_g, :] = jnp.zeros((pad_g, width), F32)

    pe_ref[pad_p:pad_p + ts, :] = sc_ref[...] * sv_ref[...]
    ge_ref[pad_g:pad_g + ts, :] = cu_ref[...] * _sigmoid(cg_ref[...])

    phases = {}
    for k in range(CF_CONV):
        off = pad_g - (CF_CONV - 1) + k
        phases.setdefault(off % sub, []).append((k, off - off % sub))
    for r0 in range(0, ts, conv_rows):
        for c0 in range(0, width, conv_lanes):
            cols = slice(c0, c0 + conv_lanes)
            acc = None
            for p, taps in sorted(phases.items()):
                nrows = conv_rows if p == 0 else conv_rows + sub
                z = None
                for k, q in taps:
                    term = cfw_ref[k:k + 1, cols] * ge_ref[r0 + q:r0 + q + nrows, cols]
                    z = term if z is None else z + term
                part = z if p == 0 else z[p:p + conv_rows]
                acc = part if acc is None else acc + part
            d_ref[r0:r0 + conv_rows, cols] = acc + cfb_ref[:, cols]

    for r0 in range(0, ts, rows):
        base = pad_p - (SC_CONV - 1) + r0
        acc = scw_ref[0:1, :] * pe_ref[base:base + rows, :]
        for k in range(1, SC_CONV):
            acc = acc + scw_ref[k:k + 1, :] * pe_ref[base + k:base + k + rows, :]
        yc_ref[r0:r0 + rows, :] = (sb_ref[r0:r0 + rows, :] * acc).astype(yc_ref.dtype)

        d = d_ref[r0:r0 + rows, :]
        mu = jnp.mean(d, axis=-1, keepdims=True)
        dc = d - mu
        var = jnp.mean(dc * dc, axis=-1, keepdims=True)
        z = (dc * lax.rsqrt(var + EPS)) * lng_ref[...] + lnb_ref[...]
        yd_ref[r0:r0 + rows, :] = (z * _sigmoid(z)).astype(yd_ref.dtype)

    pe_ref[0:pad_p, :] = pe_ref[ts:ts + pad_p, :]
    ge_ref[0:pad_g, :] = ge_ref[ts:ts + pad_g, :]


def odd_mixer(proj_out, batch, sc_w, cf_w, cf_b, ln_g, ln_b, layer, ts=256, rows=32,
              conv_rows=64, conv_lanes=256):
    T = proj_out.shape[0]
    W = proj_out.shape[1] // 5
    nt = T // batch // ts

    def col(section):
        return pl.BlockSpec((ts, W), lambda b, i: (b * nt + i, section))

    vec = pl.BlockSpec((None, 1, W), lambda b, i: (layer, 0, 0))
    out = pl.BlockSpec((ts, W), lambda b, i: (b * nt + i, 0))
    return pl.pallas_call(
        functools.partial(_odd_body, ts=ts, rows=rows, conv_rows=conv_rows, conv_lanes=conv_lanes),
        grid=(batch, nt),
        in_specs=[
            col(0), col(1), col(2), col(3), col(4),
            pl.BlockSpec((None, SC_CONV, W), lambda b, i: (layer, 0, 0)),
            pl.BlockSpec((None, CF_CONV, W), lambda b, i: (layer, 0, 0)),
            vec, vec, vec,
        ],
        out_specs=[out, out],
        out_shape=[jax.ShapeDtypeStruct((T, W), BF16), jax.ShapeDtypeStruct((T, W), BF16)],
        scratch_shapes=[
            pltpu.VMEM((ts + 8, W), F32),
            pltpu.VMEM((ts + 32, W), F32),
            pltpu.VMEM((ts, W), F32),
        ],
        compiler_params=_params("arbitrary", "arbitrary"),
        name="odd_mixer",
    )(proj_out, proj_out, proj_out, proj_out, proj_out, sc_w, cf_w, cf_b, ln_g, ln_b)


def kernel(x, ln_mix_g, ln_ffn_g, ln_final_g, ev_w_in, ev_b_in, lru_conv_w, lru_conv_b, lru_wa, lru_ba, lru_wx, lru_bx, lru_lambda, hgrn_lb_logits, hgrn_norm_g, ev_w_out, od_w_in, od_b_in, sc_conv_w, cf_conv_w, cf_conv_b, cf_ln_g, cf_ln_b, od_w_out, ffn_w_gate, ffn_w_up, ffn_w_down):
    B, S, D = x.shape
    depth = ln_mix_g.shape[0]
    xt = x.reshape(B * S, D)

    def rows(p):
        return p.reshape(p.shape[0], 1, p.shape[-1])

    ln_mix = rows(ln_mix_g)
    ln_ffn = rows(ln_ffn_g)
    ln_fin = ln_final_g.reshape(1, 1, D)
    ev_b = rows(ev_b_in)
    od_b = rows(od_b_in)
    w_gates = jnp.concatenate([lru_wa, lru_wx], axis=-1)
    n_even = lru_wa.shape[0]
    lru_ba2 = lru_ba.reshape(n_even, 1, -1)
    lru_bx2 = lru_bx.reshape(n_even, 1, -1)

    xg, ss = prenorm(xt, ln_mix, 0)
    for layer in range(depth):
        j = layer // 2
        if layer % 2 == 0:
            p = proj(xg, ss, ev_w_in, ev_b, j)
            ya = lru_mixer(p, B, lru_conv_w, rows(lru_conv_b), w_gates, lru_ba2, lru_bx2,
                           rows(lru_lambda), j)
            yb = hgrn_mixer(p, B, hgrn_lb_logits, rows(hgrn_norm_g), j)
            xt, xg, ss = out_proj(ya, yb, ev_w_out, j, xt, ln_ffn, layer)
        else:
            p = proj(xg, ss, od_w_in, od_b, j)
            yc, yd = odd_mixer(p, B, sc_conv_w, cf_conv_w, rows(cf_conv_b), rows(cf_ln_g),
                               rows(cf_ln_b), j)
            xt, xg, ss = out_proj(yc, yd, od_w_out, j, xt, ln_ffn, layer)
        a = ffn_up(xg, ss, ffn_w_gate, ffn_w_up, layer)
        if layer + 1 < depth:
            xt, xg, ss = ffn_down(a, ffn_w_down, layer, xt, ln_mix, layer + 1)
        else:
            xt = ffn_down(a, ffn_w_down, layer, xt, None, 0)
    out = rmsnorm(xt, ln_fin, 0, F32)
    return out.reshape(B, S, D)
```

```python
import functools

import jax
import jax.numpy as jnp
from jax import lax
from jax.experimental import pallas as pl
from jax.experimental.pallas import tpu as pltpu

F32 = jnp.float32
BF16 = jnp.bfloat16

EPS = 1e-6
F_FLOOR = 1e-30
LRU_C = 8.0
LRU_HEADS = 8
LRU_CONV = 4
HGRN_HEADS = 8
HEAD_DIM = 128
SC_CONV = 3
CF_CONV = 31
HGRN_SUB = 32
LOG2_E = 1.4426950408889634

V7X_LANES = 128
SS_LANES = 8
V7X_SUBLANES = 8
V7X_VMEM_BYTES = 64 * 2**20
MIB = 2**20
KERNEL_VMEM_LIMIT = V7X_VMEM_BYTES - 8 * MIB

TILE_IN = (1024, 1024)
TILE_OUT = (1024, 1024)
TILE_UP = (1024, 512)
TILE_DOWN = (512, 512)


def _params(*semantics):
    return pltpu.CompilerParams(dimension_semantics=semantics, vmem_limit_bytes=KERNEL_VMEM_LIMIT)


def _sigmoid(x):
    return jax.nn.sigmoid(x)


def _sumsq_lanes(x):
    return jnp.broadcast_to(jnp.sum(x * x, axis=-1, keepdims=True), (x.shape[0], SS_LANES))


def _row_scale(ss_ref, d_model):
    ms = jnp.sum(ss_ref[...], axis=0)[:, 0:1] * (1.0 / d_model)
    return lax.rsqrt(ms + EPS)


def _prenorm_body(x_ref, g_ref, xg_ref, ss_ref):
    x = x_ref[...]
    xg_ref[...] = (x * g_ref[...]).astype(xg_ref.dtype)
    ss_ref[...] = _sumsq_lanes(x)


def prenorm(x, g_stack, layer, tm=512):
    T, D = x.shape
    return pl.pallas_call(
        _prenorm_body,
        grid=(T // tm,),
        in_specs=[
            pl.BlockSpec((tm, D), lambda i: (i, 0)),
            pl.BlockSpec((None, 1, D), lambda i: (layer, 0, 0)),
        ],
        out_specs=[
            pl.BlockSpec((tm, D), lambda i: (i, 0)),
            pl.BlockSpec((None, tm, SS_LANES), lambda i: (0, i, 0)),
        ],
        out_shape=[jax.ShapeDtypeStruct((T, D), BF16),
                   jax.ShapeDtypeStruct((1, T, SS_LANES), F32)],
        compiler_params=_params("arbitrary"),
        name="prenorm",
    )(x, g_stack)


def _rmsnorm_body(x_ref, g_ref, o_ref):
    x = x_ref[...]
    ms = jnp.mean(x * x, axis=-1, keepdims=True)
    o_ref[...] = ((x * lax.rsqrt(ms + EPS)) * g_ref[...]).astype(o_ref.dtype)


def rmsnorm(x, g_stack, layer, out_dtype, tm=512):
    T, D = x.shape
    return pl.pallas_call(
        _rmsnorm_body,
        grid=(T // tm,),
        in_specs=[
            pl.BlockSpec((tm, D), lambda i: (i, 0)),
            pl.BlockSpec((None, 1, D), lambda i: (layer, 0, 0)),
        ],
        out_specs=pl.BlockSpec((tm, D), lambda i: (i, 0)),
        out_shape=jax.ShapeDtypeStruct((T, D), out_dtype),
        compiler_params=_params("arbitrary"),
        name="rmsnorm",
    )(x, g_stack)


def _proj_body(xg_ref, ss_ref, w_ref, b_ref, o_ref, wbf_ref, *, d_model):
    @pl.when(pl.program_id(1) == 0)
    def _():
        wbf_ref[...] = w_ref[...].astype(BF16)

    acc = jnp.dot(xg_ref[...], wbf_ref[...], preferred_element_type=F32)
    o_ref[...] = _row_scale(ss_ref, d_model) * acc + b_ref[...]


def proj(xg, ss, w_stack, b_stack, layer):
    tm, tn = TILE_IN
    T, K = xg.shape
    N = w_stack.shape[-1]
    parts = ss.shape[0]
    return pl.pallas_call(
        functools.partial(_proj_body, d_model=K),
        grid=(N // tn, T // tm),
        in_specs=[
            pl.BlockSpec((tm, K), lambda n, m: (m, 0)),
            pl.BlockSpec((parts, tm, SS_LANES), lambda n, m: (0, m, 0)),
            pl.BlockSpec((None, K, tn), lambda n, m: (layer, 0, n)),
            pl.BlockSpec((None, 1, tn), lambda n, m: (layer, 0, n)),
        ],
        out_specs=pl.BlockSpec((tm, tn), lambda n, m: (m, n)),
        out_shape=jax.ShapeDtypeStruct((T, N), F32),
        scratch_shapes=[pltpu.VMEM((K, tn), BF16)],
        compiler_params=_params("arbitrary", "arbitrary"),
        name="in_proj",
    )(xg, ss, w_stack, b_stack)


def _emit_residual(xn, gn_ref, o_ref, xg_ref, ss_ref):
    o_ref[...] = xn
    if gn_ref is not None:
        xg_ref[...] = (xn * gn_ref[...]).astype(xg_ref.dtype)
        ss_ref[...] = _sumsq_lanes(xn)


def _out_body(*refs, emit_norm):
    if emit_norm:
        ya_ref, yb_ref, wa_ref, wb_ref, x_ref, gn_ref, o_ref, xg_ref, ss_ref, wa_bf, wb_bf = refs
    else:
        ya_ref, yb_ref, wa_ref, wb_ref, x_ref, o_ref, wa_bf, wb_bf = refs
        gn_ref = xg_ref = ss_ref = None

    @pl.when(pl.program_id(1) == 0)
    def _():
        wa_bf[...] = wa_ref[...].astype(BF16)
        wb_bf[...] = wb_ref[...].astype(BF16)

    acc = jnp.dot(ya_ref[...], wa_bf[...], preferred_element_type=F32)
    acc = acc + jnp.dot(yb_ref[...], wb_bf[...], preferred_element_type=F32)
    _emit_residual(x_ref[...] + acc, gn_ref, o_ref, xg_ref, ss_ref)


def _residual_specs(T, N, tm, tn, next_gain, next_layer):
    x_spec = pl.BlockSpec((tm, tn), lambda n, m: (m, n))
    if next_gain is None:
        return [], [], x_spec, jax.ShapeDtypeStruct((T, N), F32)
    ins = [pl.BlockSpec((None, 1, tn), lambda n, m: (next_layer, 0, n))]
    outs = [x_spec, x_spec, pl.BlockSpec((None, tm, SS_LANES), lambda n, m: (n, m, 0))]
    shapes = [jax.ShapeDtypeStruct((T, N), F32), jax.ShapeDtypeStruct((T, N), BF16),
              jax.ShapeDtypeStruct((N // tn, T, SS_LANES), F32)]
    return ins, [next_gain], outs, shapes


def out_proj(ya, yb, w_stack, layer, x, next_gain, next_layer):
    tm, tn = TILE_OUT
    T, Kh = ya.shape
    N = w_stack.shape[-1]
    g_in, g_arg, outs, shapes = _residual_specs(T, N, tm, tn, next_gain, next_layer)
    return pl.pallas_call(
        functools.partial(_out_body, emit_norm=next_gain is not None),
        grid=(N // tn, T // tm),
        in_specs=[
            pl.BlockSpec((tm, Kh), lambda n, m: (m, 0)),
            pl.BlockSpec((tm, Kh), lambda n, m: (m, 0)),
            pl.BlockSpec((None, Kh, tn), lambda n, m: (layer, 0, n)),
            pl.BlockSpec((None, Kh, tn), lambda n, m: (layer, 1, n)),
            pl.BlockSpec((tm, tn), lambda n, m: (m, n)),
        ] + g_in,
        out_specs=outs,
        out_shape=shapes,
        scratch_shapes=[pltpu.VMEM((Kh, tn), BF16), pltpu.VMEM((Kh, tn), BF16)],
        compiler_params=_params("arbitrary", "arbitrary"),
        name="out_proj",
    )(ya, yb, w_stack, w_stack, x, *g_arg)


def _ffn_up_body(xg_ref, ss_ref, wg_ref, wu_ref, o_ref, wg_bf, wu_bf, *, d_model):
    @pl.when(pl.program_id(1) == 0)
    def _():
        wg_bf[...] = wg_ref[...].astype(BF16)
        wu_bf[...] = wu_ref[...].astype(BF16)

    h = xg_ref[...]
    rs = _row_scale(ss_ref, d_model)
    g = rs * jnp.dot(h, wg_bf[...], preferred_element_type=F32)
    u = rs * jnp.dot(h, wu_bf[...], preferred_element_type=F32)
    o_ref[...] = ((g * _sigmoid(g)) * u).astype(o_ref.dtype)


def ffn_up(xg, ss, wg_stack, wu_stack, layer):
    tm, tn = TILE_UP
    T, K = xg.shape
    F = wg_stack.shape[-1]
    parts = ss.shape[0]
    return pl.pallas_call(
        functools.partial(_ffn_up_body, d_model=K),
        grid=(F // tn, T // tm),
        in_specs=[
            pl.BlockSpec((tm, K), lambda n, m: (m, 0)),
            pl.BlockSpec((parts, tm, SS_LANES), lambda n, m: (0, m, 0)),
            pl.BlockSpec((None, K, tn), lambda n, m: (layer, 0, n)),
            pl.BlockSpec((None, K, tn), lambda n, m: (layer, 0, n)),
        ],
        out_specs=pl.BlockSpec((tm, tn), lambda n, m: (m, n)),
        out_shape=jax.ShapeDtypeStruct((T, F), BF16),
        scratch_shapes=[pltpu.VMEM((K, tn), BF16), pltpu.VMEM((K, tn), BF16)],
        compiler_params=_params("arbitrary", "arbitrary"),
        name="ffn_up",
    )(xg, ss, wg_stack, wu_stack)


def _ffn_down_body(*refs, emit_norm):
    if emit_norm:
        a_ref, w_ref, x_ref, gn_ref, o_ref, xg_ref, ss_ref, wbf_ref = refs
    else:
        a_ref, w_ref, x_ref, o_ref, wbf_ref = refs
        gn_ref = xg_ref = ss_ref = None

    @pl.when(pl.program_id(1) == 0)
    def _():
        wbf_ref[...] = w_ref[...].astype(BF16)

    acc = jnp.dot(a_ref[...], wbf_ref[...], preferred_element_type=F32)
    _emit_residual(x_ref[...] + acc, gn_ref, o_ref, xg_ref, ss_ref)


def ffn_down(a, w_stack, layer, x, next_gain, next_layer):
    tm, tn = TILE_DOWN
    T, F = a.shape
    N = w_stack.shape[-1]
    g_in, g_arg, outs, shapes = _residual_specs(T, N, tm, tn, next_gain, next_layer)
    return pl.pallas_call(
        functools.partial(_ffn_down_body, emit_norm=next_gain is not None),
        grid=(N // tn, T // tm),
        in_specs=[
            pl.BlockSpec((tm, F), lambda n, m: (m, 0)),
            pl.BlockSpec((None, F, tn), lambda n, m: (layer, 0, n)),
            pl.BlockSpec((tm, tn), lambda n, m: (m, n)),
        ] + g_in,
        out_specs=outs,
        out_shape=shapes,
        scratch_shapes=[pltpu.VMEM((F, tn), BF16)],
        compiler_params=_params("arbitrary", "arbitrary"),
        name="ffn_down",
    )(a, w_stack, x, *g_arg)


def _lru_body(xa_ref, gate_ref, cw_ref, cb_ref, wg_ref, ba_ref, bx_ref, lam_ref, y_ref,
              xe_ref, hc_ref, *, ts):
    i = pl.program_id(1)
    pad = V7X_SUBLANES
    width = xa_ref.shape[1]

    @pl.when(i == 0)
    def _():
        xe_ref[0:pad, :] = jnp.zeros((pad, width), F32)
        hc_ref[...] = jnp.zeros(hc_ref.shape, F32)

    xe_ref[pad:pad + ts, :] = xa_ref[...]
    xc = cw_ref[0:1, :] * xe_ref[pl.ds(pad - (LRU_CONV - 1), ts), :]
    for k in range(1, LRU_CONV):
        xc = xc + cw_ref[k:k + 1, :] * xe_ref[pl.ds(pad - (LRU_CONV - 1) + k, ts), :]
    xc = xc + cb_ref[...]
    xe_ref[0:pad, :] = xe_ref[ts:ts + pad, :]

    xcb = xc.astype(BF16)
    r_parts, i_parts = [], []
    for h in range(LRU_HEADS):
        gh = jnp.dot(xcb[:, h * HEAD_DIM:(h + 1) * HEAD_DIM], wg_ref[h].astype(BF16),
                     preferred_element_type=F32)
        r_parts.append(gh[:, :HEAD_DIM])
        i_parts.append(gh[:, HEAD_DIM:])
    r = _sigmoid(jnp.concatenate(r_parts, axis=1) + ba_ref[...])
    ig = _sigmoid(jnp.concatenate(i_parts, axis=1) + bx_ref[...])

    nl = -lam_ref[...]
    softplus = jnp.maximum(nl, 0.0) + jnp.log1p(jnp.exp(-jnp.abs(nl)))
    log_a = (-LRU_C * r) * softplus
    a = jnp.exp(log_a)
    mult = jnp.sqrt(jnp.maximum(jnp.tanh(-log_a) * (a * a + 1.0), 0.0))
    row = lax.broadcasted_iota(jnp.int32, (pad, width), 0)
    first = jnp.where(jnp.logical_and(i == 0, row == 0), 1.0, mult[:pad])
    mult = jnp.concatenate([first, mult[pad:]], axis=0)
    u = (mult * ig) * xc

    tiles = ts // pad
    a3 = a.reshape(tiles, pad, width)
    u3 = u.reshape(tiles, pad, width)
    subrow = lax.broadcasted_iota(jnp.int32, (tiles, pad, width), 1)
    d = 1
    while d < pad:
        keep = subrow >= d
        a_s = jnp.where(keep, pltpu.roll(a3, d, 1), 1.0)
        u_s = jnp.where(keep, pltpu.roll(u3, d, 1), 0.0)
        u3 = a3 * u_s + u3
        a3 = a3 * a_s
        d *= 2
    h_prev = hc_ref[0:1, :]
    h_tiles = []
    for t in range(tiles):
        h_tile = a3[t] * h_prev + u3[t]
        h_tiles.append(h_tile)
        h_prev = h_tile[pad - 1:pad, :]
    hc_ref[0:1, :] = h_prev
    hseq = jnp.concatenate(h_tiles, axis=0)

    g = gate_ref[...]
    c = 0.7978845608028654
    cdf = 0.5 * (1.0 + jnp.tanh(c * (g + 0.044715 * (g * g * g))))
    y_ref[...] = (hseq * (g * cdf)).astype(y_ref.dtype)


def lru_mixer(proj_out, batch, conv_w, conv_b, w_gates, ba, bx, lam, layer, ts=256):
    T = proj_out.shape[0]
    W = LRU_HEADS * HEAD_DIM
    nt = T // batch // ts
    vec = pl.BlockSpec((None, 1, W), lambda b, i: (layer, 0, 0))
    return pl.pallas_call(
        functools.partial(_lru_body, ts=ts),
        grid=(batch, nt),
        in_specs=[
            pl.BlockSpec((ts, W), lambda b, i: (b * nt + i, 0)),
            pl.BlockSpec((ts, W), lambda b, i: (b * nt + i, 1)),
            pl.BlockSpec((None, LRU_CONV, W), lambda b, i: (layer, 0, 0)),
            vec,
            pl.BlockSpec((None, LRU_HEADS, HEAD_DIM, 2 * HEAD_DIM), lambda b, i: (layer, 0, 0, 0)),
            vec, vec, vec,
        ],
        out_specs=pl.BlockSpec((ts, W), lambda b, i: (b * nt + i, 0)),
        out_shape=jax.ShapeDtypeStruct((T, W), BF16),
        scratch_shapes=[
            pltpu.VMEM((ts + 8, W), F32),
            pltpu.VMEM((8, W), F32),
        ],
        compiler_params=_params("arbitrary", "arbitrary"),
        name="rg_lru",
    )(proj_out, proj_out, conv_w, conv_b, w_gates, ba, bx, lam)


def _cumsum_rows(x):
    n = x.shape[0]
    row = lax.broadcasted_iota(jnp.int32, x.shape, 0)
    d = 1
    while d < n:
        x = x + jnp.where(row >= d, pltpu.roll(x, d, 0), 0.0)
        d *= 2
    return x


def _hgrn_body(q_ref, f_ref, v_ref, g_ref, lbl_ref, ng_ref, y_ref, st_ref, rows_ref, *, ts, layer):
    @pl.when(pl.program_id(0) == 0)
    def _():
        st_ref[...] = jnp.zeros(st_ref.shape, F32)

    logits = lbl_ref[...]
    e = jnp.exp(logits - jnp.max(logits, axis=0, keepdims=True))
    sm = e / jnp.sum(e, axis=0, keepdims=True)
    lb = jnp.sum(sm[0:layer + 1, :], axis=0, keepdims=True) - sm[0:1, :]
    one_m_lb = 1.0 - lb
    ng = ng_ref[...]
    n = HGRN_SUB
    sub = V7X_SUBLANES
    batch, _, width = q_ref.shape
    subrow = lax.broadcasted_iota(jnp.int32, (sub, width), 0)
    nt_dims = (((1,), (1,)), ((), ()))
    tn_dims = (((0,), (0,)), ((), ()))

    t_idx = lax.broadcasted_iota(jnp.int32, (n, n), 0)
    s_idx = lax.broadcasted_iota(jnp.int32, (n, n), 1)
    diag_mask = t_idx == s_idx
    level_mask = {}
    h = 1
    while h < n:
        level_mask[h] = jnp.logical_and(t_idx // (2 * h) == s_idx // (2 * h),
                                        jnp.logical_and((t_idx // h) % 2 == 1, (s_idx // h) % 2 == 0))
        h *= 2

    def ref_rows(b, h):
        pieces = []
        for i in range(n // sub):
            base = i * sub
            if 2 * h >= sub:
                rho = (base // (2 * h)) * (2 * h) + h - 1
                pieces.append(jnp.broadcast_to(rows_ref[b, pl.ds(rho, 1), :], (sub, width)))
            else:
                piece = None
                for c in range(sub // (2 * h)):
                    rho = base + c * 2 * h + h - 1
                    rowv = jnp.broadcast_to(rows_ref[b, pl.ds(rho, 1), :], (sub, width))
                    piece = rowv if piece is None else jnp.where(subrow >= c * 2 * h, rowv, piece)
                pieces.append(piece)
        return jnp.concatenate(pieces, axis=0)

    def sub_block(b, j):
        r0 = pl.multiple_of(j * n, n)
        f = f_ref[b, pl.ds(r0, n), :]
        q = q_ref[b, pl.ds(r0, n), :]
        v = v_ref[b, pl.ds(r0, n), :]
        sig = _sigmoid(f)
        fg = jnp.maximum(lb + one_m_lb * sig, F_FLOOR)
        k = one_m_lb * (1.0 - sig)
        qf = q * _sigmoid(q)
        bc = _cumsum_rows(jnp.log(fg)) * LOG2_E
        rows_ref[b] = bc
        b_tot = bc[n - 1:n, :]
        qd = (qf * jnp.exp2(bc)).astype(BF16)
        kd = (k * jnp.exp2(b_tot - bc)).astype(BF16)
        e_tot = jnp.exp2(b_tot)
        vb = v.astype(BF16)
        kb = k.astype(BF16)
        q01 = jnp.concatenate([qf.astype(BF16), (qf * fg).astype(BF16)], axis=0)
        levels = []
        h = n // 2
        while h >= 2:
            br = ref_rows(b, h)
            dist = bc - br
            dec = jnp.exp2(jnp.minimum(dist, -dist))
            levels.append((h, (qf * dec).astype(BF16), (k * dec).astype(BF16)))
            h //= 2

        heads = [slice(hd * HEAD_DIM, (hd + 1) * HEAD_DIM) for hd in range(HGRN_HEADS)]
        ws, inters = [], []
        for hd, sl in enumerate(heads):
            r01 = lax.dot_general(q01[:, sl], kb[:, sl], nt_dims, preferred_element_type=F32)
            w = jnp.where(diag_mask, r01[:n], 0.0)
            w = jnp.where(level_mask[1], r01[n:], w)
            for h, ql, kl in levels:
                r = lax.dot_general(ql[:, sl], kl[:, sl], nt_dims, preferred_element_type=F32)
                w = jnp.where(level_mask[h], r, w)
            ws.append(w.astype(BF16))
            st = st_ref[b, hd]
            inters.append(lax.dot_general(qd[:, sl], st.astype(BF16), nt_dims,
                                          preferred_element_type=F32))
            d_st = lax.dot_general(vb[:, sl], kd[:, sl], tn_dims, preferred_element_type=F32)
            st_ref[b, hd] = st * e_tot[:, sl] + d_st
        outs = [inters[hd] + jnp.dot(ws[hd], vb[:, sl], preferred_element_type=F32)
                for hd, sl in enumerate(heads)]
        return jnp.concatenate(outs, axis=1)

    def finish(b, j, o_all):
        r0 = pl.multiple_of(j * n, n)
        g = g_ref[b, pl.ds(r0, n), :]
        outs = []
        for hd in range(HGRN_HEADS):
            o = o_all[:, hd * HEAD_DIM:(hd + 1) * HEAD_DIM]
            outs.append(o * lax.rsqrt(jnp.mean(o * o, axis=-1, keepdims=True) + EPS))
        o_n = jnp.concatenate(outs, axis=1) * ng
        y_ref[b, pl.ds(r0, n), :] = (o_n * (g * _sigmoid(g))).astype(y_ref.dtype)

    def step(j, o_prev):
        for b in range(batch):
            finish(b, j - 1, o_prev[b])
        return tuple(sub_block(b, j) for b in range(batch))

    n_sub = ts // n
    o_last = lax.fori_loop(1, n_sub, step, tuple(sub_block(b, 0) for b in range(batch)))
    for b in range(batch):
        finish(b, n_sub - 1, o_last[b])


def hgrn_mixer(proj_out, batch, lb_logits, norm_g, layer, ts=256):
    T = proj_out.shape[0]
    W = HGRN_HEADS * HEAD_DIM
    seq = T // batch
    n_layers = lb_logits.shape[0]
    p3 = proj_out.reshape(batch, seq, proj_out.shape[1])

    def col(section):
        return pl.BlockSpec((batch, ts, W), lambda i: (0, i, section))

    y = pl.pallas_call(
        functools.partial(_hgrn_body, ts=ts, layer=layer),
        grid=(seq // ts,),
        in_specs=[
            col(2), col(3), col(4), col(5),
            pl.BlockSpec((n_layers, W), lambda i: (0, 0)),
            pl.BlockSpec((None, 1, W), lambda i: (layer, 0, 0)),
        ],
        out_specs=pl.BlockSpec((batch, ts, W), lambda i: (0, i, 0)),
        out_shape=jax.ShapeDtypeStruct((batch, seq, W), BF16),
        scratch_shapes=[
            pltpu.VMEM((batch, HGRN_HEADS, HEAD_DIM, HEAD_DIM), F32),
            pltpu.VMEM((batch, HGRN_SUB, W), F32),
        ],
        compiler_params=_params("arbitrary"),
        name="hgrn2",
    )(p3, p3, p3, p3, lb_logits, norm_g)
    return y.reshape(T, W)


def _odd_body(sb_ref, sc_ref, sv_ref, cu_ref, cg_ref, scw_ref, cfw_ref, cfb_ref, lng_ref, lnb_ref,
              yc_ref, yd_ref, pe_ref, ge_ref, d_ref, *, ts, rows, conv_rows, conv_lanes):
    i = pl.program_id(1)
    width = sb_ref.shape[1]
    sub = V7X_SUBLANES
    pad_p = sub
    pad_g = 4 * sub

    @pl.when(i == 0)
    def _():
        pe_ref[0:pad_p, :] = jnp.zeros((pad_p, width), F32)
        ge_ref[0:pad_g, :] = jnp.zeros((pad_g, width), F32)

    pe_ref[pad_p:pad_p + ts, :] = sc_ref[...] * sv_ref[...]
    ge_ref[pad_g:pad_g + ts, :] = cu_ref[...] * _sigmoid(cg_ref[...])

    phases = {}
    for k in range(CF_CONV):
        off = pad_g - (CF_CONV - 1) + k
        phases.setdefault(off % sub, []).append((k, off - off % sub))
    for r0 in range(0, ts, conv_rows):
        for c0 in range(0, width, conv_lanes):
            cols = slice(c0, c0 + conv_lanes)
            acc = None
            for p, taps in sorted(phases.items()):
                nrows = conv_rows if p == 0 else conv_rows + sub
                z = None
                for k, q in taps:
                    term = cfw_ref[k:k + 1, cols] * ge_ref[r0 + q:r0 + q + nrows, cols]
                    z = term if z is None else z + term
                part = z if p == 0 else z[p:p + conv_rows]
                acc = part if acc is None else acc + part
            d_ref[r0:r0 + conv_rows, cols] = acc + cfb_ref[:, cols]

    for r0 in range(0, ts, rows):
        base = pad_p - (SC_CONV - 1) + r0
        acc = scw_ref[0:1, :] * pe_ref[base:base + rows, :]
        for k in range(1, SC_CONV):
            acc = acc + scw_ref[k:k + 1, :] * pe_ref[base + k:base + k + rows, :]
        yc_ref[r0:r0 + rows, :] = (sb_ref[r0:r0 + rows, :] * acc).astype(yc_ref.dtype)

        d = d_ref[r0:r0 + rows, :]
        mu = jnp.mean(d, axis=-1, keepdims=True)
        dc = d - mu
        var = jnp.mean(dc * dc, axis=-1, keepdims=True)
        z = (dc * lax.rsqrt(var + EPS)) * lng_ref[...] + lnb_ref[...]
        yd_ref[r0:r0 + rows, :] = (z * _sigmoid(z)).astype(yd_ref.dtype)

    pe_ref[0:pad_p, :] = pe_ref[ts:ts + pad_p, :]
    ge_ref[0:pad_g, :] = ge_ref[ts:ts + pad_g, :]


def odd_mixer(proj_out, batch, sc_w, cf_w, cf_b, ln_g, ln_b, layer, ts=256, rows=32,
              conv_rows=64, conv_lanes=256):
    T = proj_out.shape[0]
    W = proj_out.shape[1] // 5
    nt = T // batch // ts

    def col(section):
        return pl.BlockSpec((ts, W), lambda b, i: (b * nt + i, section))

    vec = pl.BlockSpec((None, 1, W), lambda b, i: (layer, 0, 0))
    out = pl.BlockSpec((ts, W), lambda b, i: (b * nt + i, 0))
    return pl.pallas_call(
        functools.partial(_odd_body, ts=ts, rows=rows, conv_rows=conv_rows, conv_lanes=conv_lanes),
        grid=(batch, nt),
        in_specs=[
            col(0), col(1), col(2), col(3), col(4),
            pl.BlockSpec((None, SC_CONV, W), lambda b, i: (layer, 0, 0)),
            pl.BlockSpec((None, CF_CONV, W), lambda b, i: (layer, 0, 0)),
            vec, vec, vec,
        ],
        out_specs=[out, out],
        out_shape=[jax.ShapeDtypeStruct((T, W), BF16), jax.ShapeDtypeStruct((T, W), BF16)],
        scratch_shapes=[
            pltpu.VMEM((ts + 8, W), F32),
            pltpu.VMEM((ts + 32, W), F32),
            pltpu.VMEM((ts, W), F32),
        ],
        compiler_params=_params("arbitrary", "arbitrary"),
        name="odd_mixer",
    )(proj_out, proj_out, proj_out, proj_out, proj_out, sc_w, cf_w, cf_b, ln_g, ln_b)


def kernel(x, ln_mix_g, ln_ffn_g, ln_final_g, ev_w_in, ev_b_in, lru_conv_w, lru_conv_b, lru_wa, lru_ba, lru_wx, lru_bx, lru_lambda, hgrn_lb_logits, hgrn_norm_g, ev_w_out, od_w_in, od_b_in, sc_conv_w, cf_conv_w, cf_conv_b, cf_ln_g, cf_ln_b, od_w_out, ffn_w_gate, ffn_w_up, ffn_w_down):
    B, S, D = x.shape
    depth = ln_mix_g.shape[0]
    xt = x.reshape(B * S, D)

    def rows(p):
        return p.reshape(p.shape[0], 1, p.shape[-1])

    ln_mix = rows(ln_mix_g)
    ln_ffn = rows(ln_ffn_g)
    ln_fin = ln_final_g.reshape(1, 1, D)
    ev_b = rows(ev_b_in)
    od_b = rows(od_b_in)
    w_gates = jnp.concatenate([lru_wa, lru_wx], axis=-1)
    n_even = lru_wa.shape[0]
    lru_ba2 = lru_ba.reshape(n_even, 1, -1)
    lru_bx2 = lru_bx.reshape(n_even, 1, -1)

    xg, ss = prenorm(xt, ln_mix, 0)
    for layer in range(depth):
        j = layer // 2
        if layer % 2 == 0:
            p = proj(xg, ss, ev_w_in, ev_b, j)
            ya = lru_mixer(p, B, lru_conv_w, rows(lru_conv_b), w_gates, lru_ba2, lru_bx2,
                           rows(lru_lambda), j)
            yb = hgrn_mixer(p, B, hgrn_lb_logits, rows(hgrn_norm_g), j)
            xt, xg, ss = out_proj(ya, yb, ev_w_out, j, xt, ln_ffn, layer)
        else:
            p = proj(xg, ss, od_w_in, od_b, j)
            yc, yd = odd_mixer(p, B, sc_conv_w, cf_conv_w, rows(cf_conv_b), rows(cf_ln_g),
                               rows(cf_ln_b), j)
            xt, xg, ss = out_proj(yc, yd, od_w_out, j, xt, ln_ffn, layer)
        a = ffn_up(xg, ss, ffn_w_gate, ffn_w_up, layer)
        if layer + 1 < depth:
            xt, xg, ss = ffn_down(a, ffn_w_down, layer, xt, ln_mix, layer + 1)
        else:
            xt = ffn_down(a, ffn_w_down, layer, xt, None, 0)
    out = rmsnorm(xt, ln_fin, 0, F32)
    return out.reshape(B, S, D)
```

```python
import functools

import jax
import jax.numpy as jnp
from jax import lax
from jax.experimental import pallas as pl
from jax.experimental.pallas import tpu as pltpu

F32 = jnp.float32
BF16 = jnp.bfloat16

EPS = 1e-6
F_FLOOR = 1e-30
LRU_C = 8.0
LRU_HEADS = 8
LRU_CONV = 4
HGRN_HEADS = 8
HEAD_DIM = 128
SC_CONV = 3
CF_CONV = 31
HGRN_SUB = 32
LOG2_E = 1.4426950408889634

V7X_LANES = 128
SS_LANES = 8
V7X_SUBLANES = 8
V7X_VMEM_BYTES = 64 * 2**20
MIB = 2**20
KERNEL_VMEM_LIMIT = V7X_VMEM_BYTES - 8 * MIB

TILE_IN = (1024, 1024)
TILE_OUT = (1024, 1024)
TILE_UP = (1024, 512)
TILE_DOWN = (512, 512)


def _params(*semantics):
    return pltpu.CompilerParams(dimension_semantics=semantics, vmem_limit_bytes=KERNEL_VMEM_LIMIT)


def _sigmoid(x):
    return jax.nn.sigmoid(x)


def _sumsq_lanes(x):
    return jnp.broadcast_to(jnp.sum(x * x, axis=-1, keepdims=True), (x.shape[0], SS_LANES))


def _row_scale(ss_ref, d_model):
    ms = jnp.sum(ss_ref[...], axis=0)[:, 0:1] * (1.0 / d_model)
    return lax.rsqrt(ms + EPS)


def _prenorm_body(x_ref, g_ref, xg_ref, ss_ref):
    x = x_ref[...]
    xg_ref[...] = (x * g_ref[...]).astype(xg_ref.dtype)
    ss_ref[...] = _sumsq_lanes(x)


def prenorm(x, g_stack, layer, tm=512):
    T, D = x.shape
    return pl.pallas_call(
        _prenorm_body,
        grid=(T // tm,),
        in_specs=[
            pl.BlockSpec((tm, D), lambda i: (i, 0)),
            pl.BlockSpec((None, 1, D), lambda i: (layer, 0, 0)),
        ],
        out_specs=[
            pl.BlockSpec((tm, D), lambda i: (i, 0)),
            pl.BlockSpec((None, tm, SS_LANES), lambda i: (0, i, 0)),
        ],
        out_shape=[jax.ShapeDtypeStruct((T, D), BF16),
                   jax.ShapeDtypeStruct((1, T, SS_LANES), F32)],
        compiler_params=_params("arbitrary"),
        name="prenorm",
    )(x, g_stack)


def _rmsnorm_body(x_ref, g_ref, o_ref):
    x = x_ref[...]
    ms = jnp.mean(x * x, axis=-1, keepdims=True)
    o_ref[...] = ((x * lax.rsqrt(ms + EPS)) * g_ref[...]).astype(o_ref.dtype)


def rmsnorm(x, g_stack, layer, out_dtype, tm=512):
    T, D = x.shape
    return pl.pallas_call(
        _rmsnorm_body,
        grid=(T // tm,),
        in_specs=[
            pl.BlockSpec((tm, D), lambda i: (i, 0)),
            pl.BlockSpec((None, 1, D), lambda i: (layer, 0, 0)),
        ],
        out_specs=pl.BlockSpec((tm, D), lambda i: (i, 0)),
        out_shape=jax.ShapeDtypeStruct((T, D), out_dtype),
        compiler_params=_params("arbitrary"),
        name="rmsnorm",
    )(x, g_stack)


def _proj_body(xg_ref, ss_ref, w_ref, b_ref, o_ref, wbf_ref, *, d_model):
    @pl.when(pl.program_id(1) == 0)
    def _():
        wbf_ref[...] = w_ref[...].astype(BF16)

    acc = jnp.dot(xg_ref[...], wbf_ref[...], preferred_element_type=F32)
    o_ref[...] = _row_scale(ss_ref, d_model) * acc + b_ref[...]


def proj(xg, ss, w_stack, b_stack, layer):
    tm, tn = TILE_IN
    T, K = xg.shape
    N = w_stack.shape[-1]
    parts = ss.shape[0]
    return pl.pallas_call(
        functools.partial(_proj_body, d_model=K),
        grid=(N // tn, T // tm),
        in_specs=[
            pl.BlockSpec((tm, K), lambda n, m: (m, 0)),
            pl.BlockSpec((parts, tm, SS_LANES), lambda n, m: (0, m, 0)),
            pl.BlockSpec((None, K, tn), lambda n, m: (layer, 0, n)),
            pl.BlockSpec((None, 1, tn), lambda n, m: (layer, 0, n)),
        ],
        out_specs=pl.BlockSpec((tm, tn), lambda n, m: (m, n)),
        out_shape=jax.ShapeDtypeStruct((T, N), F32),
        scratch_shapes=[pltpu.VMEM((K, tn), BF16)],
        compiler_params=_params("arbitrary", "arbitrary"),
        name="in_proj",
    )(xg, ss, w_stack, b_stack)


def _emit_residual(xn, gn_ref, o_ref, xg_ref, ss_ref):
    o_ref[...] = xn
    if gn_ref is not None:
        xg_ref[...] = (xn * gn_ref[...]).astype(xg_ref.dtype)
        ss_ref[...] = _sumsq_lanes(xn)


def _out_body(*refs, emit_norm):
    if emit_norm:
        ya_ref, yb_ref, wa_ref, wb_ref, x_ref, gn_ref, o_ref, xg_ref, ss_ref, wa_bf, wb_bf = refs
    else:
        ya_ref, yb_ref, wa_ref, wb_ref, x_ref, o_ref, wa_bf, wb_bf = refs
        gn_ref = xg_ref = ss_ref = None

    @pl.when(pl.program_id(1) == 0)
    def _():
        wa_bf[...] = wa_ref[...].astype(BF16)
        wb_bf[...] = wb_ref[...].astype(BF16)

    acc = jnp.dot(ya_ref[...], wa_bf[...], preferred_element_type=F32)
    acc = acc + jnp.dot(yb_ref[...], wb_bf[...], preferred_element_type=F32)
    _emit_residual(x_ref[...] + acc, gn_ref, o_ref, xg_ref, ss_ref)


def _residual_specs(T, N, tm, tn, next_gain, next_layer):
    x_spec = pl.BlockSpec((tm, tn), lambda n, m: (m, n))
    if next_gain is None:
        return [], [], x_spec, jax.ShapeDtypeStruct((T, N), F32)
    ins = [pl.BlockSpec((None, 1, tn), lambda n, m: (next_layer, 0, n))]
    outs = [x_spec, x_spec, pl.BlockSpec((None, tm, SS_LANES), lambda n, m: (n, m, 0))]
    shapes = [jax.ShapeDtypeStruct((T, N), F32), jax.ShapeDtypeStruct((T, N), BF16),
              jax.ShapeDtypeStruct((N // tn, T, SS_LANES), F32)]
    return ins, [next_gain], outs, shapes


def out_proj(ya, yb, w_stack, layer, x, next_gain, next_layer):
    tm, tn = TILE_OUT
    T, Kh = ya.shape
    N = w_stack.shape[-1]
    g_in, g_arg, outs, shapes = _residual_specs(T, N, tm, tn, next_gain, next_layer)
    return pl.pallas_call(
        functools.partial(_out_body, emit_norm=next_gain is not None),
        grid=(N // tn, T // tm),
        in_specs=[
            pl.BlockSpec((tm, Kh), lambda n, m: (m, 0)),
            pl.BlockSpec((tm, Kh), lambda n, m: (m, 0)),
            pl.BlockSpec((None, Kh, tn), lambda n, m: (layer, 0, n)),
            pl.BlockSpec((None, Kh, tn), lambda n, m: (layer, 1, n)),
            pl.BlockSpec((tm, tn), lambda n, m: (m, n)),
        ] + g_in,
        out_specs=outs,
        out_shape=shapes,
        scratch_shapes=[pltpu.VMEM((Kh, tn), BF16), pltpu.VMEM((Kh, tn), BF16)],
        compiler_params=_params("arbitrary", "arbitrary"),
        name="out_proj",
    )(ya, yb, w_stack, w_stack, x, *g_arg)


def _ffn_up_body(xg_ref, ss_ref, wg_ref, wu_ref, o_ref, wg_bf, wu_bf, *, d_model):
    @pl.when(pl.program_id(1) == 0)
    def _():
        wg_bf[...] = wg_ref[...].astype(BF16)
        wu_bf[...] = wu_ref[...].astype(BF16)

    h = xg_ref[...]
    rs = _row_scale(ss_ref, d_model)
    g = rs * jnp.dot(h, wg_bf[...], preferred_element_type=F32)
    u = rs * jnp.dot(h, wu_bf[...], preferred_element_type=F32)
    o_ref[...] = ((g * _sigmoid(g)) * u).astype(o_ref.dtype)


def ffn_up(xg, ss, wg_stack, wu_stack, layer):
    tm, tn = TILE_UP
    T, K = xg.shape
    F = wg_stack.shape[-1]
    parts = ss.shape[0]
    return pl.pallas_call(
        functools.partial(_ffn_up_body, d_model=K),
        grid=(F // tn, T // tm),
        in_specs=[
            pl.BlockSpec((tm, K), lambda n, m: (m, 0)),
            pl.BlockSpec((parts, tm, SS_LANES), lambda n, m: (0, m, 0)),
            pl.BlockSpec((None, K, tn), lambda n, m: (layer, 0, n)),
            pl.BlockSpec((None, K, tn), lambda n, m: (layer, 0, n)),
        ],
        out_specs=pl.BlockSpec((tm, tn), lambda n, m: (m, n)),
        out_shape=jax.ShapeDtypeStruct((T, F), BF16),
        scratch_shapes=[pltpu.VMEM((K, tn), BF16), pltpu.VMEM((K, tn), BF16)],
        compiler_params=_params("arbitrary", "arbitrary"),
        name="ffn_up",
    )(xg, ss, wg_stack, wu_stack)


def _ffn_down_body(*refs, emit_norm):
    if emit_norm:
        a_ref, w_ref, x_ref, gn_ref, o_ref, xg_ref, ss_ref, wbf_ref = refs
    else:
        a_ref, w_ref, x_ref, o_ref, wbf_ref = refs
        gn_ref = xg_ref = ss_ref = None

    @pl.when(pl.program_id(1) == 0)
    def _():
        wbf_ref[...] = w_ref[...].astype(BF16)

    acc = jnp.dot(a_ref[...], wbf_ref[...], preferred_element_type=F32)
    _emit_residual(x_ref[...] + acc, gn_ref, o_ref, xg_ref, ss_ref)


def ffn_down(a, w_stack, layer, x, next_gain, next_layer):
    tm, tn = TILE_DOWN
    T, F = a.shape
    N = w_stack.shape[-1]
    g_in, g_arg, outs, shapes = _residual_specs(T, N, tm, tn, next_gain, next_layer)
    return pl.pallas_call(
        functools.partial(_ffn_down_body, emit_norm=next_gain is not None),
        grid=(N // tn, T // tm),
        in_specs=[
            pl.BlockSpec((tm, F), lambda n, m: (m, 0)),
            pl.BlockSpec((None, F, tn), lambda n, m: (layer, 0, n)),
            pl.BlockSpec((tm, tn), lambda n, m: (m, n)),
        ] + g_in,
        out_specs=outs,
        out_shape=shapes,
        scratch_shapes=[pltpu.VMEM((F, tn), BF16)],
        compiler_params=_params("arbitrary", "arbitrary"),
        name="ffn_down",
    )(a, w_stack, x, *g_arg)


def _lru_body(xa_ref, gate_ref, cw_ref, cb_ref, wg_ref, ba_ref, bx_ref, lam_ref, y_ref,
              xe_ref, hc_ref, *, ts):
    i = pl.program_id(1)
    pad = V7X_SUBLANES
    width = xa_ref.shape[1]

    @pl.when(i == 0)
    def _():
        xe_ref[0:pad, :] = jnp.zeros((pad, width), F32)
        hc_ref[...] = jnp.zeros(hc_ref.shape, F32)

    xe_ref[pad:pad + ts, :] = xa_ref[...]
    xc = cw_ref[0:1, :] * xe_ref[pl.ds(pad - (LRU_CONV - 1), ts), :]
    for k in range(1, LRU_CONV):
        xc = xc + cw_ref[k:k + 1, :] * xe_ref[pl.ds(pad - (LRU_CONV - 1) + k, ts), :]
    xc = xc + cb_ref[...]
    xe_ref[0:pad, :] = xe_ref[ts:ts + pad, :]

    xcb = xc.astype(BF16)
    r_parts, i_parts = [], []
    for h in range(LRU_HEADS):
        gh = jnp.dot(xcb[:, h * HEAD_DIM:(h + 1) * HEAD_DIM], wg_ref[h].astype(BF16),
                     preferred_element_type=F32)
        r_parts.append(gh[:, :HEAD_DIM])
        i_parts.append(gh[:, HEAD_DIM:])
    r = _sigmoid(jnp.concatenate(r_parts, axis=1) + ba_ref[...])
    ig = _sigmoid(jnp.concatenate(i_parts, axis=1) + bx_ref[...])

    nl = -lam_ref[...]
    softplus = jnp.maximum(nl, 0.0) + jnp.log1p(jnp.exp(-jnp.abs(nl)))
    log_a = (-LRU_C * r) * softplus
    a = jnp.exp(log_a)
    mult = jnp.sqrt(jnp.maximum(jnp.tanh(-log_a) * (a * a + 1.0), 0.0))
    row = lax.broadcasted_iota(jnp.int32, (pad, width), 0)
    first = jnp.where(jnp.logical_and(i == 0, row == 0), 1.0, mult[:pad])
    mult = jnp.concatenate([first, mult[pad:]], axis=0)
    u = (mult * ig) * xc

    tiles = ts // pad
    a3 = a.reshape(tiles, pad, width)
    u3 = u.reshape(tiles, pad, width)
    subrow = lax.broadcasted_iota(jnp.int32, (tiles, pad, width), 1)
    d = 1
    while d < pad:
        keep = subrow >= d
        a_s = jnp.where(keep, pltpu.roll(a3, d, 1), 1.0)
        u_s = jnp.where(keep, pltpu.roll(u3, d, 1), 0.0)
        u3 = a3 * u_s + u3
        a3 = a3 * a_s
        d *= 2
    h_prev = hc_ref[0:1, :]
    h_tiles = []
    for t in range(tiles):
        h_tile = a3[t] * h_prev + u3[t]
        h_tiles.append(h_tile)
        h_prev = h_tile[pad - 1:pad, :]
    hc_ref[0:1, :] = h_prev
    hseq = jnp.concatenate(h_tiles, axis=0)

    g = gate_ref[...]
    c = 0.7978845608028654
    cdf = 0.5 * (1.0 + jnp.tanh(c * (g + 0.044715 * (g * g * g))))
    y_ref[...] = (hseq * (g * cdf)).astype(y_ref.dtype)


def lru_mixer(proj_out, batch, conv_w, conv_b, w_gates, ba, bx, lam, layer, ts=256):
    T = proj_out.shape[0]
    W = LRU_HEADS * HEAD_DIM
    nt = T // batch // ts
    vec = pl.BlockSpec((None, 1, W), lambda b, i: (layer, 0, 0))
    return pl.pallas_call(
        functools.partial(_lru_body, ts=ts),
        grid=(batch, nt),
        in_specs=[
            pl.BlockSpec((ts, W), lambda b, i: (b * nt + i, 0)),
            pl.BlockSpec((ts, W), lambda b, i: (b * nt + i, 1)),
            pl.BlockSpec((None, LRU_CONV, W), lambda b, i: (layer, 0, 0)),
            vec,
            pl.BlockSpec((None, LRU_HEADS, HEAD_DIM, 2 * HEAD_DIM), lambda b, i: (layer, 0, 0, 0)),
            vec, vec, vec,
        ],
        out_specs=pl.BlockSpec((ts, W), lambda b, i: (b * nt + i, 0)),
        out_shape=jax.ShapeDtypeStruct((T, W), BF16),
        scratch_shapes=[
            pltpu.VMEM((ts + 8, W), F32),
            pltpu.VMEM((8, W), F32),
        ],
        compiler_params=_params("arbitrary", "arbitrary"),
        name="rg_lru",
    )(proj_out, proj_out, conv_w, conv_b, w_gates, ba, bx, lam)


def _cumsum_rows(x):
    n = x.shape[0]
    row = lax.broadcasted_iota(jnp.int32, x.shape, 0)
    d = 1
    while d < n:
        x = x + jnp.where(row >= d, pltpu.roll(x, d, 0), 0.0)
        d *= 2
    return x


def _hgrn_body(q_ref, f_ref, v_ref, g_ref, lbl_ref, ng_ref, y_ref, st_ref, rows_ref, *, ts, layer):
    @pl.when(pl.program_id(0) == 0)
    def _():
        st_ref[...] = jnp.zeros(st_ref.shape, F32)

    logits = lbl_ref[...]
    e = jnp.exp(logits - jnp.max(logits, axis=0, keepdims=True))
    sm = e / jnp.sum(e, axis=0, keepdims=True)
    lb = jnp.sum(sm[0:layer + 1, :], axis=0, keepdims=True) - sm[0:1, :]
    one_m_lb = 1.0 - lb
    ng = ng_ref[...]
    n = HGRN_SUB
    sub = V7X_SUBLANES
    batch, _, width = q_ref.shape
    subrow = lax.broadcasted_iota(jnp.int32, (sub, width), 0)
    nt_dims = (((1,), (1,)), ((), ()))
    tn_dims = (((0,), (0,)), ((), ()))

    t_idx = lax.broadcasted_iota(jnp.int32, (n, n), 0)
    s_idx = lax.broadcasted_iota(jnp.int32, (n, n), 1)
    diag_mask = t_idx == s_idx
    level_mask = {}
    h = 1
    while h < n:
        level_mask[h] = jnp.logical_and(t_idx // (2 * h) == s_idx // (2 * h),
                                        jnp.logical_and((t_idx // h) % 2 == 1, (s_idx // h) % 2 == 0))
        h *= 2

    def ref_rows(b, h):
        pieces = []
        for i in range(n // sub):
            base = i * sub
            if 2 * h >= sub:
                rho = (base // (2 * h)) * (2 * h) + h - 1
                pieces.append(jnp.broadcast_to(rows_ref[b, pl.ds(rho, 1), :], (sub, width)))
            else:
                piece = None
                for c in range(sub // (2 * h)):
                    rho = base + c * 2 * h + h - 1
                    rowv = jnp.broadcast_to(rows_ref[b, pl.ds(rho, 1), :], (sub, width))
                    piece = rowv if piece is None else jnp.where(subrow >= c * 2 * h, rowv, piece)
                pieces.append(piece)
        return jnp.concatenate(pieces, axis=0)

    def sub_block(b, j):
        r0 = pl.multiple_of(j * n, n)
        f = f_ref[b, pl.ds(r0, n), :]
        q = q_ref[b, pl.ds(r0, n), :]
        v = v_ref[b, pl.ds(r0, n), :]
        sig = _sigmoid(f)
        fg = jnp.maximum(lb + one_m_lb * sig, F_FLOOR)
        k = one_m_lb * (1.0 - sig)
        qf = q * _sigmoid(q)
        bc = _cumsum_rows(jnp.log(fg)) * LOG2_E
        rows_ref[b] = bc
        b_tot = bc[n - 1:n, :]
        qd = (qf * jnp.exp2(bc)).astype(BF16)
        kd = (k * jnp.exp2(b_tot - bc)).astype(BF16)
        e_tot = jnp.exp2(b_tot)
        vb = v.astype(BF16)
        kb = k.astype(BF16)
        q01 = jnp.concatenate([qf.astype(BF16), (qf * fg).astype(BF16)], axis=0)
        levels = []
        h = n // 2
        while h >= 2:
            br = ref_rows(b, h)
            dist = bc - br
            dec = jnp.exp2(jnp.minimum(dist, -dist))
            levels.append((h, (qf * dec).astype(BF16), (k * dec).astype(BF16)))
            h //= 2

        heads = [slice(hd * HEAD_DIM, (hd + 1) * HEAD_DIM) for hd in range(HGRN_HEADS)]
        ws, inters = [], []
        for hd, sl in enumerate(heads):
            r01 = lax.dot_general(q01[:, sl], kb[:, sl], nt_dims, preferred_element_type=F32)
            w = jnp.where(diag_mask, r01[:n], 0.0)
            w = jnp.where(level_mask[1], r01[n:], w)
            for h, ql, kl in levels:
                r = lax.dot_general(ql[:, sl], kl[:, sl], nt_dims, preferred_element_type=F32)
                w = jnp.where(level_mask[h], r, w)
            ws.append(w.astype(BF16))
            st = st_ref[b, hd]
            inters.append(lax.dot_general(qd[:, sl], st.astype(BF16), nt_dims,
                                          preferred_element_type=F32))
            d_st = lax.dot_general(vb[:, sl], kd[:, sl], tn_dims, preferred_element_type=F32)
            st_ref[b, hd] = st * e_tot[:, sl] + d_st
        outs = [inters[hd] + jnp.dot(ws[hd], vb[:, sl], preferred_element_type=F32)
                for hd, sl in enumerate(heads)]
        return jnp.concatenate(outs, axis=1)

    def finish(b, j, o_all):
        r0 = pl.multiple_of(j * n, n)
        g = g_ref[b, pl.ds(r0, n), :]
        outs = []
        for hd in range(HGRN_HEADS):
            o = o_all[:, hd * HEAD_DIM:(hd + 1) * HEAD_DIM]
            outs.append(o * lax.rsqrt(jnp.mean(o * o, axis=-1, keepdims=True) + EPS))
        o_n = jnp.concatenate(outs, axis=1) * ng
        y_ref[b, pl.ds(r0, n), :] = (o_n * (g * _sigmoid(g))).astype(y_ref.dtype)

    def step(j, o_prev):
        for b in range(batch):
            finish(b, j - 1, o_prev[b])
        return tuple(sub_block(b, j) for b in range(batch))

    n_sub = ts // n
    o_last = lax.fori_loop(1, n_sub, step, tuple(sub_block(b, 0) for b in range(batch)))
    for b in range(batch):
        finish(b, n_sub - 1, o_last[b])


def hgrn_mixer(proj_out, batch, lb_logits, norm_g, layer, ts=256):
    T = proj_out.shape[0]
    W = HGRN_HEADS * HEAD_DIM
    seq = T // batch
    n_layers = lb_logits.shape[0]
    p3 = proj_out.reshape(batch, seq, proj_out.shape[1])

    def col(section):
        return pl.BlockSpec((batch, ts, W), lambda i: (0, i, section))

    y = pl.pallas_call(
        functools.partial(_hgrn_body, ts=ts, layer=layer),
        grid=(seq // ts,),
        in_specs=[
            col(2), col(3), col(4), col(5),
            pl.BlockSpec((n_layers, W), lambda i: (0, 0)),
            pl.BlockSpec((None, 1, W), lambda i: (layer, 0, 0)),
        ],
        out_specs=pl.BlockSpec((batch, ts, W), lambda i: (0, i, 0)),
        out_shape=jax.ShapeDtypeStruct((batch, seq, W), BF16),
        scratch_shapes=[
            pltpu.VMEM((batch, HGRN_HEADS, HEAD_DIM, HEAD_DIM), F32),
            pltpu.VMEM((batch, HGRN_SUB, W), F32),
        ],
        compiler_params=_params("arbitrary"),
        name="hgrn2",
    )(p3, p3, p3, p3, lb_logits, norm_g)
    return y.reshape(T, W)


OUT_GROUP = 256
ODD_PAD_P = V7X_SUBLANES
ODD_PAD_G = 4 * V7X_SUBLANES


def _odd_reset(pe_ref, ge_ref):
    pe_ref[0:ODD_PAD_P, :] = jnp.zeros((ODD_PAD_P, pe_ref.shape[1]), F32)
    ge_ref[0:ODD_PAD_G, :] = jnp.zeros((ODD_PAD_G, ge_ref.shape[1]), F32)


def _odd_tile(sb_ref, sc_ref, sv_ref, cu_ref, cg_ref, scw_ref, cfw_ref, cfb_ref, lng_ref, lnb_ref,
              pe_ref, ge_ref, d_ref, store, after_conv_block, *, ts, rows, conv_rows, conv_lanes):
    width = sb_ref.shape[1]
    sub = V7X_SUBLANES
    pad_p = ODD_PAD_P
    pad_g = ODD_PAD_G

    pe_ref[pad_p:pad_p + ts, :] = sc_ref[...] * sv_ref[...]
    ge_ref[pad_g:pad_g + ts, :] = cu_ref[...] * _sigmoid(cg_ref[...])

    phases = {}
    for k in range(CF_CONV):
        off = pad_g - (CF_CONV - 1) + k
        phases.setdefault(off % sub, []).append((k, off - off % sub))
    for r0 in range(0, ts, conv_rows):
        for c0 in range(0, width, conv_lanes):
            cols = slice(c0, c0 + conv_lanes)
            acc = None
            for p, taps in sorted(phases.items()):
                nrows = conv_rows if p == 0 else conv_rows + sub
                z = None
                for k, q in taps:
                    term = cfw_ref[k:k + 1, cols] * ge_ref[r0 + q:r0 + q + nrows, cols]
                    z = term if z is None else z + term
                part = z if p == 0 else z[p:p + conv_rows]
                acc = part if acc is None else acc + part
            d_ref[r0:r0 + conv_rows, cols] = acc + cfb_ref[:, cols]
            after_conv_block((r0 // conv_rows) * (width // conv_lanes) + c0 // conv_lanes)

    for r0 in range(0, ts, rows):
        base = pad_p - (SC_CONV - 1) + r0
        acc = scw_ref[0:1, :] * pe_ref[base:base + rows, :]
        for k in range(1, SC_CONV):
            acc = acc + scw_ref[k:k + 1, :] * pe_ref[base + k:base + k + rows, :]
        y_c = sb_ref[r0:r0 + rows, :] * acc

        d = d_ref[r0:r0 + rows, :]
        mu = jnp.mean(d, axis=-1, keepdims=True)
        dc = d - mu
        var = jnp.mean(dc * dc, axis=-1, keepdims=True)
        z = (dc * lax.rsqrt(var + EPS)) * lng_ref[...] + lnb_ref[...]
        store(r0, y_c, z * _sigmoid(z))

    pe_ref[0:pad_p, :] = pe_ref[ts:ts + pad_p, :]
    ge_ref[0:pad_g, :] = ge_ref[ts:ts + pad_g, :]


def _odd_out_body(sb_ref, sc_ref, sv_ref, cu_ref, cg_ref, scw_ref, cfw_ref, cfb_ref, lng_ref, lnb_ref,
                  w_ref, x_ref, gn_ref, o_ref, xg_ref, ss_ref, pe_ref, ge_ref, d_ref, y_ref,
                  *, ts, rows, conv_rows, conv_lanes):
    i = pl.program_id(1)
    width = sb_ref.shape[1]

    @pl.when(i == 0)
    def _():
        y_ref[...] = jnp.zeros(y_ref.shape, y_ref.dtype)
        _odd_reset(pe_ref, ge_ref)

    n_conv = (ts // conv_rows) * (width // conv_lanes)
    n_groups = o_ref.shape[1] // OUT_GROUP
    every = n_conv // n_groups
    sumsq = []

    def after_conv_block(k):
        if k % every != 0:
            return
        cols = slice((k // every) * OUT_GROUP, (k // every + 1) * OUT_GROUP)
        xn = x_ref[:, cols] + jnp.dot(y_ref[...], w_ref[:, cols], preferred_element_type=F32)
        o_ref[:, cols] = xn
        xg_ref[:, cols] = (xn * gn_ref[:, cols]).astype(xg_ref.dtype)
        sumsq.append(jnp.sum(xn * xn, axis=-1, keepdims=True))

    def store(r0, y_c, y_d):
        y_ref[r0:r0 + rows, 0:width] = y_c.astype(y_ref.dtype)
        y_ref[r0:r0 + rows, width:2 * width] = y_d.astype(y_ref.dtype)

    _odd_tile(sb_ref, sc_ref, sv_ref, cu_ref, cg_ref, scw_ref, cfw_ref, cfb_ref, lng_ref, lnb_ref,
              pe_ref, ge_ref, d_ref, store, after_conv_block, ts=ts, rows=rows,
              conv_rows=conv_rows, conv_lanes=conv_lanes)
    ss_ref[...] = jnp.broadcast_to(sum(sumsq), ss_ref.shape)


def odd_mixer_out(proj_out, batch, sc_w, cf_w, cf_b, ln_g, ln_b, layer, w_bf, x, next_gain, next_layer,
                  ts=256, rows=32, conv_rows=64, conv_lanes=256):
    T = proj_out.shape[0]
    W = proj_out.shape[1] // 5
    D = x.shape[1]
    nt = T // batch // ts

    def col(section):
        return pl.BlockSpec((ts, W), lambda b, i: (b * nt + jnp.minimum(i, nt - 1), section))

    def prev(b, i):
        return b * nt + jnp.maximum(i - 1, 0)

    vec = pl.BlockSpec((None, 1, W), lambda b, i: (layer, 0, 0))
    row_blk = pl.BlockSpec((ts, D), lambda b, i: (prev(b, i), 0))
    return pl.pallas_call(
        functools.partial(_odd_out_body, ts=ts, rows=rows, conv_rows=conv_rows, conv_lanes=conv_lanes),
        grid=(batch, nt + 1),
        in_specs=[
            col(0), col(1), col(2), col(3), col(4),
            pl.BlockSpec((None, SC_CONV, W), lambda b, i: (layer, 0, 0)),
            pl.BlockSpec((None, CF_CONV, W), lambda b, i: (layer, 0, 0)),
            vec, vec, vec,
            pl.BlockSpec((None, 2 * W, D), lambda b, i: (layer, 0, 0), pipeline_mode=pl.Buffered(1)),
            row_blk,
            pl.BlockSpec((None, 1, D), lambda b, i: (next_layer, 0, 0)),
        ],
        out_specs=[row_blk, row_blk,
                   pl.BlockSpec((None, ts, SS_LANES), lambda b, i: (0, prev(b, i), 0))],
        out_shape=[jax.ShapeDtypeStruct((T, D), F32), jax.ShapeDtypeStruct((T, D), BF16),
                   jax.ShapeDtypeStruct((1, T, SS_LANES), F32)],
        scratch_shapes=[
            pltpu.VMEM((ts + 8, W), F32),
            pltpu.VMEM((ts + 32, W), F32),
            pltpu.VMEM((ts, W), F32),
            pltpu.VMEM((ts, 2 * W), BF16),
        ],
        compiler_params=_params("arbitrary", "arbitrary"),
        name="odd_mixer_out",
    )(proj_out, proj_out, proj_out, proj_out, proj_out, sc_w, cf_w, cf_b, ln_g, ln_b, w_bf, x, next_gain)


def _cast_body(x_ref, o_ref):
    o_ref[...] = x_ref[...].astype(o_ref.dtype)


def cast_bf16(w, rows=512):
    L, K, N = w.shape
    return pl.pallas_call(
        _cast_body,
        grid=(L, K // rows),
        in_specs=[pl.BlockSpec((None, rows, N), lambda l, k: (l, k, 0))],
        out_specs=pl.BlockSpec((None, rows, N), lambda l, k: (l, k, 0)),
        out_shape=jax.ShapeDtypeStruct((L, K, N), BF16),
        compiler_params=_params("arbitrary", "arbitrary"),
        name="cast_bf16",
    )(w)


def kernel(x, ln_mix_g, ln_ffn_g, ln_final_g, ev_w_in, ev_b_in, lru_conv_w, lru_conv_b, lru_wa, lru_ba, lru_wx, lru_bx, lru_lambda, hgrn_lb_logits, hgrn_norm_g, ev_w_out, od_w_in, od_b_in, sc_conv_w, cf_conv_w, cf_conv_b, cf_ln_g, cf_ln_b, od_w_out, ffn_w_gate, ffn_w_up, ffn_w_down):
    B, S, D = x.shape
    depth = ln_mix_g.shape[0]
    xt = x.reshape(B * S, D)

    def rows(p):
        return p.reshape(p.shape[0], 1, p.shape[-1])

    ln_mix = rows(ln_mix_g)
    ln_ffn = rows(ln_ffn_g)
    ln_fin = ln_final_g.reshape(1, 1, D)
    ev_b = rows(ev_b_in)
    od_b = rows(od_b_in)
    w_gates = jnp.concatenate([lru_wa, lru_wx], axis=-1)
    n_even = lru_wa.shape[0]
    lru_ba2 = lru_ba.reshape(n_even, 1, -1)
    lru_bx2 = lru_bx.reshape(n_even, 1, -1)

    od_w_out_bf = cast_bf16(od_w_out)
    xg, ss = prenorm(xt, ln_mix, 0)
    for layer in range(depth):
        j = layer // 2
        if layer % 2 == 0:
            p = proj(xg, ss, ev_w_in, ev_b, j)
            ya = lru_mixer(p, B, lru_conv_w, rows(lru_conv_b), w_gates, lru_ba2, lru_bx2,
                           rows(lru_lambda), j)
            yb = hgrn_mixer(p, B, hgrn_lb_logits, rows(hgrn_norm_g), j)
            xt, xg, ss = out_proj(ya, yb, ev_w_out, j, xt, ln_ffn, layer)
        else:
            p = proj(xg, ss, od_w_in, od_b, j)
            xt, xg, ss = odd_mixer_out(p, B, sc_conv_w, cf_conv_w, rows(cf_conv_b), rows(cf_ln_g),
                                       rows(cf_ln_b), j, od_w_out_bf, xt, ln_ffn, layer)
        a = ffn_up(xg, ss, ffn_w_gate, ffn_w_up, layer)
        if layer + 1 < depth:
            xt, xg, ss = ffn_down(a, ffn_w_down, layer, xt, ln_mix, layer + 1)
        else:
            xt = ffn_down(a, ffn_w_down, layer, xt, None, 0)
    out = rmsnorm(xt, ln_fin, 0, F32)
    return out.reshape(B, S, D)
```

```python
import functools

import jax
import jax.numpy as jnp
from jax import lax
from jax.experimental import pallas as pl
from jax.experimental.pallas import tpu as pltpu

F32 = jnp.float32
BF16 = jnp.bfloat16

EPS = 1e-6
F_FLOOR = 1e-30
LRU_C = 8.0
LRU_HEADS = 8
LRU_CONV = 4
HGRN_HEADS = 8
HEAD_DIM = 128
SC_CONV = 3
CF_CONV = 31
HGRN_SUB = 32
LOG2_E = 1.4426950408889634

V7X_LANES = 128
SS_LANES = 8
V7X_SUBLANES = 8
V7X_VMEM_BYTES = 64 * 2**20
MIB = 2**20
KERNEL_VMEM_LIMIT = V7X_VMEM_BYTES - 8 * MIB

TILE_IN = (1024, 1024)
TILE_OUT = (512, 2048)
TILE_UP = (1024, 512)
TILE_DOWN = (512, 512)


def _params(*semantics):
    return pltpu.CompilerParams(dimension_semantics=semantics, vmem_limit_bytes=KERNEL_VMEM_LIMIT)


def _sigmoid(x):
    return jax.nn.sigmoid(x)


def _sumsq_lanes(x):
    return jnp.broadcast_to(jnp.sum(x * x, axis=-1, keepdims=True), (x.shape[0], SS_LANES))


def _row_scale(ss_ref, d_model):
    ms = jnp.sum(ss_ref[...], axis=0)[:, 0:1] * (1.0 / d_model)
    return lax.rsqrt(ms + EPS)


def _prenorm_body(x_ref, g_ref, xg_ref, ss_ref):
    x = x_ref[...]
    xg_ref[...] = (x * g_ref[...]).astype(xg_ref.dtype)
    ss_ref[...] = _sumsq_lanes(x)


def prenorm(x, g_stack, layer, tm=512):
    T, D = x.shape
    return pl.pallas_call(
        _prenorm_body,
        grid=(T // tm,),
        in_specs=[
            pl.BlockSpec((tm, D), lambda i: (i, 0)),
            pl.BlockSpec((None, 1, D), lambda i: (layer, 0, 0)),
        ],
        out_specs=[
            pl.BlockSpec((tm, D), lambda i: (i, 0)),
            pl.BlockSpec((None, tm, SS_LANES), lambda i: (0, i, 0)),
        ],
        out_shape=[jax.ShapeDtypeStruct((T, D), BF16),
                   jax.ShapeDtypeStruct((1, T, SS_LANES), F32)],
        compiler_params=_params("arbitrary"),
        name="prenorm",
    )(x, g_stack)


def _rmsnorm_body(x_ref, g_ref, o_ref):
    x = x_ref[...]
    ms = jnp.mean(x * x, axis=-1, keepdims=True)
    o_ref[...] = ((x * lax.rsqrt(ms + EPS)) * g_ref[...]).astype(o_ref.dtype)


def rmsnorm(x, g_stack, layer, out_dtype, tm=512):
    T, D = x.shape
    return pl.pallas_call(
        _rmsnorm_body,
        grid=(T // tm,),
        in_specs=[
            pl.BlockSpec((tm, D), lambda i: (i, 0)),
            pl.BlockSpec((None, 1, D), lambda i: (layer, 0, 0)),
        ],
        out_specs=pl.BlockSpec((tm, D), lambda i: (i, 0)),
        out_shape=jax.ShapeDtypeStruct((T, D), out_dtype),
        compiler_params=_params("arbitrary"),
        name="rmsnorm",
    )(x, g_stack)


def _proj_body(xg_ref, ss_ref, w_ref, b_ref, o_ref, wbf_ref, *, d_model):
    @pl.when(pl.program_id(1) == 0)
    def _():
        wbf_ref[...] = w_ref[...].astype(BF16)

    acc = jnp.dot(xg_ref[...], wbf_ref[...], preferred_element_type=F32)
    o_ref[...] = _row_scale(ss_ref, d_model) * acc + b_ref[...]


def proj(xg, ss, w_stack, b_stack, layer):
    tm, tn = TILE_IN
    T, K = xg.shape
    N = w_stack.shape[-1]
    parts = ss.shape[0]
    return pl.pallas_call(
        functools.partial(_proj_body, d_model=K),
        grid=(N // tn, T // tm),
        in_specs=[
            pl.BlockSpec((tm, K), lambda n, m: (m, 0)),
            pl.BlockSpec((parts, tm, SS_LANES), lambda n, m: (0, m, 0)),
            pl.BlockSpec((None, K, tn), lambda n, m: (layer, 0, n)),
            pl.BlockSpec((None, 1, tn), lambda n, m: (layer, 0, n)),
        ],
        out_specs=pl.BlockSpec((tm, tn), lambda n, m: (m, n)),
        out_shape=jax.ShapeDtypeStruct((T, N), F32),
        scratch_shapes=[pltpu.VMEM((K, tn), BF16)],
        compiler_params=_params("arbitrary", "arbitrary"),
        name="in_proj",
    )(xg, ss, w_stack, b_stack)


def _emit_residual(xn, gn_ref, o_ref, xg_ref, ss_ref):
    o_ref[...] = xn
    if gn_ref is not None:
        xg_ref[...] = (xn * gn_ref[...]).astype(xg_ref.dtype)
        ss_ref[...] = _sumsq_lanes(xn)


def _out_body(*refs, emit_norm):
    if emit_norm:
        ya_ref, yb_ref, wa_ref, wb_ref, x_ref, gn_ref, o_ref, xg_ref, ss_ref, wa_bf, wb_bf = refs
    else:
        ya_ref, yb_ref, wa_ref, wb_ref, x_ref, o_ref, wa_bf, wb_bf = refs
        gn_ref = xg_ref = ss_ref = None

    @pl.when(pl.program_id(1) == 0)
    def _():
        wa_bf[...] = wa_ref[...].astype(BF16)
        wb_bf[...] = wb_ref[...].astype(BF16)

    acc = jnp.dot(ya_ref[...], wa_bf[...], preferred_element_type=F32)
    acc = acc + jnp.dot(yb_ref[...], wb_bf[...], preferred_element_type=F32)
    _emit_residual(x_ref[...] + acc, gn_ref, o_ref, xg_ref, ss_ref)


def _residual_specs(T, N, tm, tn, next_gain, next_layer):
    x_spec = pl.BlockSpec((tm, tn), lambda n, m: (m, n))
    if next_gain is None:
        return [], [], x_spec, jax.ShapeDtypeStruct((T, N), F32)
    ins = [pl.BlockSpec((None, 1, tn), lambda n, m: (next_layer, 0, n))]
    outs = [x_spec, x_spec, pl.BlockSpec((None, tm, SS_LANES), lambda n, m: (n, m, 0))]
    shapes = [jax.ShapeDtypeStruct((T, N), F32), jax.ShapeDtypeStruct((T, N), BF16),
              jax.ShapeDtypeStruct((N // tn, T, SS_LANES), F32)]
    return ins, [next_gain], outs, shapes


def out_proj(ya, yb, w_stack, layer, x, next_gain, next_layer):
    tm, tn = TILE_OUT
    T, Kh = ya.shape
    N = w_stack.shape[-1]
    g_in, g_arg, outs, shapes = _residual_specs(T, N, tm, tn, next_gain, next_layer)
    return pl.pallas_call(
        functools.partial(_out_body, emit_norm=next_gain is not None),
        grid=(N // tn, T // tm),
        in_specs=[
            pl.BlockSpec((tm, Kh), lambda n, m: (m, 0)),
            pl.BlockSpec((tm, Kh), lambda n, m: (m, 0)),
            pl.BlockSpec((None, Kh, tn), lambda n, m: (layer, 0, n), pipeline_mode=pl.Buffered(1)),
            pl.BlockSpec((None, Kh, tn), lambda n, m: (layer, 1, n), pipeline_mode=pl.Buffered(1)),
            pl.BlockSpec((tm, tn), lambda n, m: (m, n)),
        ] + g_in,
        out_specs=outs,
        out_shape=shapes,
        scratch_shapes=[pltpu.VMEM((Kh, tn), BF16), pltpu.VMEM((Kh, tn), BF16)],
        compiler_params=_params("arbitrary", "arbitrary"),
        name="out_proj",
    )(ya, yb, w_stack, w_stack, x, *g_arg)


def _ffn_up_body(xg_ref, ss_ref, wg_ref, wu_ref, o_ref, wg_bf, wu_bf, *, d_model):
    @pl.when(pl.program_id(1) == 0)
    def _():
        wg_bf[...] = wg_ref[...].astype(BF16)
        wu_bf[...] = wu_ref[...].astype(BF16)

    h = xg_ref[...]
    rs = _row_scale(ss_ref, d_model)
    g = rs * jnp.dot(h, wg_bf[...], preferred_element_type=F32)
    u = rs * jnp.dot(h, wu_bf[...], preferred_element_type=F32)
    o_ref[...] = ((g * _sigmoid(g)) * u).astype(o_ref.dtype)


def ffn_up(xg, ss, wg_stack, wu_stack, layer):
    tm, tn = TILE_UP
    T, K = xg.shape
    F = wg_stack.shape[-1]
    parts = ss.shape[0]
    return pl.pallas_call(
        functools.partial(_ffn_up_body, d_model=K),
        grid=(F // tn, T // tm),
        in_specs=[
            pl.BlockSpec((tm, K), lambda n, m: (m, 0)),
            pl.BlockSpec((parts, tm, SS_LANES), lambda n, m: (0, m, 0)),
            pl.BlockSpec((None, K, tn), lambda n, m: (layer, 0, n)),
            pl.BlockSpec((None, K, tn), lambda n, m: (layer, 0, n)),
        ],
        out_specs=pl.BlockSpec((tm, tn), lambda n, m: (m, n)),
        out_shape=jax.ShapeDtypeStruct((T, F), BF16),
        scratch_shapes=[pltpu.VMEM((K, tn), BF16), pltpu.VMEM((K, tn), BF16)],
        compiler_params=_params("arbitrary", "arbitrary"),
        name="ffn_up",
    )(xg, ss, wg_stack, wu_stack)


def _ffn_down_body(*refs, emit_norm):
    if emit_norm:
        a_ref, w_ref, x_ref, gn_ref, o_ref, xg_ref, ss_ref, wbf_ref = refs
    else:
        a_ref, w_ref, x_ref, o_ref, wbf_ref = refs
        gn_ref = xg_ref = ss_ref = None

    @pl.when(pl.program_id(1) == 0)
    def _():
        wbf_ref[...] = w_ref[...].astype(BF16)

    acc = jnp.dot(a_ref[...], wbf_ref[...], preferred_element_type=F32)
    _emit_residual(x_ref[...] + acc, gn_ref, o_ref, xg_ref, ss_ref)


def ffn_down(a, w_stack, layer, x, next_gain, next_layer):
    tm, tn = TILE_DOWN
    T, F = a.shape
    N = w_stack.shape[-1]
    g_in, g_arg, outs, shapes = _residual_specs(T, N, tm, tn, next_gain, next_layer)
    return pl.pallas_call(
        functools.partial(_ffn_down_body, emit_norm=next_gain is not None),
        grid=(N // tn, T // tm),
        in_specs=[
            pl.BlockSpec((tm, F), lambda n, m: (m, 0)),
            pl.BlockSpec((None, F, tn), lambda n, m: (layer, 0, n)),
            pl.BlockSpec((tm, tn), lambda n, m: (m, n)),
        ] + g_in,
        out_specs=outs,
        out_shape=shapes,
        scratch_shapes=[pltpu.VMEM((F, tn), BF16)],
        compiler_params=_params("arbitrary", "arbitrary"),
        name="ffn_down",
    )(a, w_stack, x, *g_arg)


def _lru_body(xa_ref, gate_ref, cw_ref, cb_ref, wg_ref, ba_ref, bx_ref, lam_ref, y_ref,
              xe_ref, hc_ref, *, ts):
    i = pl.program_id(1)
    pad = V7X_SUBLANES
    width = xa_ref.shape[1]

    @pl.when(i == 0)
    def _():
        xe_ref[0:pad, :] = jnp.zeros((pad, width), F32)
        hc_ref[...] = jnp.zeros(hc_ref.shape, F32)

    xe_ref[pad:pad + ts, :] = xa_ref[...]
    xc = cw_ref[0:1, :] * xe_ref[pl.ds(pad - (LRU_CONV - 1), ts), :]
    for k in range(1, LRU_CONV):
        xc = xc + cw_ref[k:k + 1, :] * xe_ref[pl.ds(pad - (LRU_CONV - 1) + k, ts), :]
    xc = xc + cb_ref[...]
    xe_ref[0:pad, :] = xe_ref[ts:ts + pad, :]

    xcb = xc.astype(BF16)
    r_parts, i_parts = [], []
    for h in range(LRU_HEADS):
        gh = jnp.dot(xcb[:, h * HEAD_DIM:(h + 1) * HEAD_DIM], wg_ref[h].astype(BF16),
                     preferred_element_type=F32)
        r_parts.append(gh[:, :HEAD_DIM])
        i_parts.append(gh[:, HEAD_DIM:])
    r = _sigmoid(jnp.concatenate(r_parts, axis=1) + ba_ref[...])
    ig = _sigmoid(jnp.concatenate(i_parts, axis=1) + bx_ref[...])

    nl = -lam_ref[...]
    softplus = jnp.maximum(nl, 0.0) + jnp.log1p(jnp.exp(-jnp.abs(nl)))
    log_a = (-LRU_C * r) * softplus
    a = jnp.exp(log_a)
    mult = jnp.sqrt(jnp.maximum(jnp.tanh(-log_a) * (a * a + 1.0), 0.0))
    row = lax.broadcasted_iota(jnp.int32, (pad, width), 0)
    first = jnp.where(jnp.logical_and(i == 0, row == 0), 1.0, mult[:pad])
    mult = jnp.concatenate([first, mult[pad:]], axis=0)
    u = (mult * ig) * xc

    tiles = ts // pad
    a3 = a.reshape(tiles, pad, width)
    u3 = u.reshape(tiles, pad, width)
    subrow = lax.broadcasted_iota(jnp.int32, (tiles, pad, width), 1)
    d = 1
    while d < pad:
        keep = subrow >= d
        a_s = jnp.where(keep, pltpu.roll(a3, d, 1), 1.0)
        u_s = jnp.where(keep, pltpu.roll(u3, d, 1), 0.0)
        u3 = a3 * u_s + u3
        a3 = a3 * a_s
        d *= 2
    h_prev = hc_ref[0:1, :]
    h_tiles = []
    for t in range(tiles):
        h_tile = a3[t] * h_prev + u3[t]
        h_tiles.append(h_tile)
        h_prev = h_tile[pad - 1:pad, :]
    hc_ref[0:1, :] = h_prev
    hseq = jnp.concatenate(h_tiles, axis=0)

    g = gate_ref[...]
    c = 0.7978845608028654
    cdf = 0.5 * (1.0 + jnp.tanh(c * (g + 0.044715 * (g * g * g))))
    y_ref[...] = (hseq * (g * cdf)).astype(y_ref.dtype)


def lru_mixer(proj_out, batch, conv_w, conv_b, w_gates, ba, bx, lam, layer, ts=256):
    T = proj_out.shape[0]
    W = LRU_HEADS * HEAD_DIM
    nt = T // batch // ts
    vec = pl.BlockSpec((None, 1, W), lambda b, i: (layer, 0, 0))
    return pl.pallas_call(
        functools.partial(_lru_body, ts=ts),
        grid=(batch, nt),
        in_specs=[
            pl.BlockSpec((ts, W), lambda b, i: (b * nt + i, 0)),
            pl.BlockSpec((ts, W), lambda b, i: (b * nt + i, 1)),
            pl.BlockSpec((None, LRU_CONV, W), lambda b, i: (layer, 0, 0)),
            vec,
            pl.BlockSpec((None, LRU_HEADS, HEAD_DIM, 2 * HEAD_DIM), lambda b, i: (layer, 0, 0, 0)),
            vec, vec, vec,
        ],
        out_specs=pl.BlockSpec((ts, W), lambda b, i: (b * nt + i, 0)),
        out_shape=jax.ShapeDtypeStruct((T, W), BF16),
        scratch_shapes=[
            pltpu.VMEM((ts + 8, W), F32),
            pltpu.VMEM((8, W), F32),
        ],
        compiler_params=_params("arbitrary", "arbitrary"),
        name="rg_lru",
    )(proj_out, proj_out, conv_w, conv_b, w_gates, ba, bx, lam)


def _cumsum_rows(x):
    n = x.shape[0]
    row = lax.broadcasted_iota(jnp.int32, x.shape, 0)
    d = 1
    while d < n:
        x = x + jnp.where(row >= d, pltpu.roll(x, d, 0), 0.0)
        d *= 2
    return x


def _hgrn_body(q_ref, f_ref, v_ref, g_ref, lbl_ref, ng_ref, y_ref, st_ref, rows_ref, *, ts, layer):
    @pl.when(pl.program_id(0) == 0)
    def _():
        st_ref[...] = jnp.zeros(st_ref.shape, F32)

    logits = lbl_ref[...]
    e = jnp.exp(logits - jnp.max(logits, axis=0, keepdims=True))
    sm = e / jnp.sum(e, axis=0, keepdims=True)
    lb = jnp.sum(sm[0:layer + 1, :], axis=0, keepdims=True) - sm[0:1, :]
    one_m_lb = 1.0 - lb
    ng = ng_ref[...]
    n = HGRN_SUB
    sub = V7X_SUBLANES
    batch, _, width = q_ref.shape
    subrow = lax.broadcasted_iota(jnp.int32, (sub, width), 0)
    nt_dims = (((1,), (1,)), ((), ()))
    tn_dims = (((0,), (0,)), ((), ()))

    t_idx = lax.broadcasted_iota(jnp.int32, (n, n), 0)
    s_idx = lax.broadcasted_iota(jnp.int32, (n, n), 1)
    diag_mask = t_idx == s_idx
    level_mask = {}
    h = 1
    while h < n:
        level_mask[h] = jnp.logical_and(t_idx // (2 * h) == s_idx // (2 * h),
                                        jnp.logical_and((t_idx // h) % 2 == 1, (s_idx // h) % 2 == 0))
        h *= 2

    def ref_rows(b, h):
        pieces = []
        for i in range(n // sub):
            base = i * sub
            if 2 * h >= sub:
                rho = (base // (2 * h)) * (2 * h) + h - 1
                pieces.append(jnp.broadcast_to(rows_ref[b, pl.ds(rho, 1), :], (sub, width)))
            else:
                piece = None
                for c in range(sub // (2 * h)):
                    rho = base + c * 2 * h + h - 1
                    rowv = jnp.broadcast_to(rows_ref[b, pl.ds(rho, 1), :], (sub, width))
                    piece = rowv if piece is None else jnp.where(subrow >= c * 2 * h, rowv, piece)
                pieces.append(piece)
        return jnp.concatenate(pieces, axis=0)

    def sub_block(b, j):
        r0 = pl.multiple_of(j * n, n)
        f = f_ref[b, pl.ds(r0, n), :]
        q = q_ref[b, pl.ds(r0, n), :]
        v = v_ref[b, pl.ds(r0, n), :]
        sig = _sigmoid(f)
        fg = jnp.maximum(lb + one_m_lb * sig, F_FLOOR)
        k = one_m_lb * (1.0 - sig)
        qf = q * _sigmoid(q)
        bc = _cumsum_rows(jnp.log(fg)) * LOG2_E
        rows_ref[b] = bc
        b_tot = bc[n - 1:n, :]
        qd = (qf * jnp.exp2(bc)).astype(BF16)
        kd = (k * jnp.exp2(b_tot - bc)).astype(BF16)
        e_tot = jnp.exp2(b_tot)
        vb = v.astype(BF16)
        kb = k.astype(BF16)
        q01 = jnp.concatenate([qf.astype(BF16), (qf * fg).astype(BF16)], axis=0)
        levels = []
        h = n // 2
        while h >= 2:
            br = ref_rows(b, h)
            dist = bc - br
            dec = jnp.exp2(jnp.minimum(dist, -dist))
            levels.append((h, (qf * dec).astype(BF16), (k * dec).astype(BF16)))
            h //= 2

        heads = [slice(hd * HEAD_DIM, (hd + 1) * HEAD_DIM) for hd in range(HGRN_HEADS)]
        ws, inters = [], []
        for hd, sl in enumerate(heads):
            r01 = lax.dot_general(q01[:, sl], kb[:, sl], nt_dims, preferred_element_type=F32)
            w = jnp.where(diag_mask, r01[:n], 0.0)
            w = jnp.where(level_mask[1], r01[n:], w)
            for h, ql, kl in levels:
                r = lax.dot_general(ql[:, sl], kl[:, sl], nt_dims, preferred_element_type=F32)
                w = jnp.where(level_mask[h], r, w)
            ws.append(w.astype(BF16))
            st = st_ref[b, hd]
            inters.append(lax.dot_general(qd[:, sl], st.astype(BF16), nt_dims,
                                          preferred_element_type=F32))
            d_st = lax.dot_general(vb[:, sl], kd[:, sl], tn_dims, preferred_element_type=F32)
            st_ref[b, hd] = st * e_tot[:, sl] + d_st
        outs = [inters[hd] + jnp.dot(ws[hd], vb[:, sl], preferred_element_type=F32)
                for hd, sl in enumerate(heads)]
        return jnp.concatenate(outs, axis=1)

    def finish(b, j, o_all):
        r0 = pl.multiple_of(j * n, n)
        g = g_ref[b, pl.ds(r0, n), :]
        outs = []
        for hd in range(HGRN_HEADS):
            o = o_all[:, hd * HEAD_DIM:(hd + 1) * HEAD_DIM]
            outs.append(o * lax.rsqrt(jnp.mean(o * o, axis=-1, keepdims=True) + EPS))
        o_n = jnp.concatenate(outs, axis=1) * ng
        y_ref[b, pl.ds(r0, n), :] = (o_n * (g * _sigmoid(g))).astype(y_ref.dtype)

    def step(j, o_prev):
        for b in range(batch):
            finish(b, j - 1, o_prev[b])
        return tuple(sub_block(b, j) for b in range(batch))

    n_sub = ts // n
    o_last = lax.fori_loop(1, n_sub, step, tuple(sub_block(b, 0) for b in range(batch)))
    for b in range(batch):
        finish(b, n_sub - 1, o_last[b])


def hgrn_mixer(proj_out, batch, lb_logits, norm_g, layer, ts=512):
    T = proj_out.shape[0]
    W = HGRN_HEADS * HEAD_DIM
    seq = T // batch
    n_layers = lb_logits.shape[0]
    p3 = proj_out.reshape(batch, seq, proj_out.shape[1])

    def col(section):
        return pl.BlockSpec((batch, ts, W), lambda i: (0, i, section))

    y = pl.pallas_call(
        functools.partial(_hgrn_body, ts=ts, layer=layer),
        grid=(seq // ts,),
        in_specs=[
            col(2), col(3), col(4), col(5),
            pl.BlockSpec((n_layers, W), lambda i: (0, 0)),
            pl.BlockSpec((None, 1, W), lambda i: (layer, 0, 0)),
        ],
        out_specs=pl.BlockSpec((batch, ts, W), lambda i: (0, i, 0)),
        out_shape=jax.ShapeDtypeStruct((batch, seq, W), BF16),
        scratch_shapes=[
            pltpu.VMEM((batch, HGRN_HEADS, HEAD_DIM, HEAD_DIM), F32),
            pltpu.VMEM((batch, HGRN_SUB, W), F32),
        ],
        compiler_params=_params("arbitrary"),
        name="hgrn2",
    )(p3, p3, p3, p3, lb_logits, norm_g)
    return y.reshape(T, W)


ODD_PAD_P = V7X_SUBLANES
ODD_PAD_G = 4 * V7X_SUBLANES


def _odd_reset(pe_ref, ge_ref):
    pe_ref[0:ODD_PAD_P, :] = jnp.zeros((ODD_PAD_P, pe_ref.shape[1]), F32)
    ge_ref[0:ODD_PAD_G, :] = jnp.zeros((ODD_PAD_G, ge_ref.shape[1]), F32)


def _odd_tile(sb_ref, sc_ref, sv_ref, cu_ref, cg_ref, scw_ref, cfw_ref, cfb_ref, lng_ref, lnb_ref,
              pe_ref, ge_ref, d_ref, store, *, ts, rows, conv_rows, conv_lanes):
    width = sb_ref.shape[1]
    sub = V7X_SUBLANES
    pad_p = ODD_PAD_P
    pad_g = ODD_PAD_G

    pe_ref[pad_p:pad_p + ts, :] = sc_ref[...] * sv_ref[...]
    ge_ref[pad_g:pad_g + ts, :] = cu_ref[...] * _sigmoid(cg_ref[...])

    phases = {}
    for k in range(CF_CONV):
        off = pad_g - (CF_CONV - 1) + k
        phases.setdefault(off % sub, []).append((k, off - off % sub))
    for r0 in range(0, ts, conv_rows):
        for c0 in range(0, width, conv_lanes):
            cols = slice(c0, c0 + conv_lanes)
            acc = None
            for p, taps in sorted(phases.items()):
                nrows = conv_rows if p == 0 else conv_rows + sub
                z = None
                for k, q in taps:
                    term = cfw_ref[k:k + 1, cols] * ge_ref[r0 + q:r0 + q + nrows, cols]
                    z = term if z is None else z + term
                part = z if p == 0 else z[p:p + conv_rows]
                acc = part if acc is None else acc + part
            d_ref[r0:r0 + conv_rows, cols] = acc + cfb_ref[:, cols]

    for r0 in range(0, ts, rows):
        base = pad_p - (SC_CONV - 1) + r0
        acc = scw_ref[0:1, :] * pe_ref[base:base + rows, :]
        for k in range(1, SC_CONV):
            acc = acc + scw_ref[k:k + 1, :] * pe_ref[base + k:base + k + rows, :]
        y_c = sb_ref[r0:r0 + rows, :] * acc

        d = d_ref[r0:r0 + rows, :]
        mu = jnp.mean(d, axis=-1, keepdims=True)
        dc = d - mu
        var = jnp.mean(dc * dc, axis=-1, keepdims=True)
        z = (dc * lax.rsqrt(var + EPS)) * lng_ref[...] + lnb_ref[...]
        store(r0, y_c, z * _sigmoid(z))

    pe_ref[0:pad_p, :] = pe_ref[ts:ts + pad_p, :]
    ge_ref[0:pad_g, :] = ge_ref[ts:ts + pad_g, :]


def _odd_body(sb_ref, sc_ref, sv_ref, cu_ref, cg_ref, scw_ref, cfw_ref, cfb_ref, lng_ref, lnb_ref,
              yc_ref, yd_ref, pe_ref, ge_ref, d_ref, *, ts, rows, conv_rows, conv_lanes):
    @pl.when(pl.program_id(1) == 0)
    def _():
        _odd_reset(pe_ref, ge_ref)

    def store(r0, y_c, y_d):
        yc_ref[r0:r0 + rows, :] = y_c.astype(yc_ref.dtype)
        yd_ref[r0:r0 + rows, :] = y_d.astype(yd_ref.dtype)

    _odd_tile(sb_ref, sc_ref, sv_ref, cu_ref, cg_ref, scw_ref, cfw_ref, cfb_ref, lng_ref, lnb_ref,
              pe_ref, ge_ref, d_ref, store, ts=ts, rows=rows, conv_rows=conv_rows,
              conv_lanes=conv_lanes)


def odd_mixer(proj_out, batch, sc_w, cf_w, cf_b, ln_g, ln_b, layer, ts=256, rows=32,
              conv_rows=64, conv_lanes=256):
    T = proj_out.shape[0]
    W = proj_out.shape[1] // 5
    nt = T // batch // ts

    def col(section):
        return pl.BlockSpec((ts, W), lambda b, i: (b * nt + i, section))

    vec = pl.BlockSpec((None, 1, W), lambda b, i: (layer, 0, 0))
    out = pl.BlockSpec((ts, W), lambda b, i: (b * nt + i, 0))
    return pl.pallas_call(
        functools.partial(_odd_body, ts=ts, rows=rows, conv_rows=conv_rows, conv_lanes=conv_lanes),
        grid=(batch, nt),
        in_specs=[
            col(0), col(1), col(2), col(3), col(4),
            pl.BlockSpec((None, SC_CONV, W), lambda b, i: (layer, 0, 0)),
            pl.BlockSpec((None, CF_CONV, W), lambda b, i: (layer, 0, 0)),
            vec, vec, vec,
        ],
        out_specs=[out, out],
        out_shape=[jax.ShapeDtypeStruct((T, W), BF16), jax.ShapeDtypeStruct((T, W), BF16)],
        scratch_shapes=[
            pltpu.VMEM((ts + 8, W), F32),
            pltpu.VMEM((ts + 32, W), F32),
            pltpu.VMEM((ts, W), F32),
        ],
        compiler_params=_params("arbitrary", "arbitrary"),
        name="odd_mixer",
    )(proj_out, proj_out, proj_out, proj_out, proj_out, sc_w, cf_w, cf_b, ln_g, ln_b)


def kernel(x, ln_mix_g, ln_ffn_g, ln_final_g, ev_w_in, ev_b_in, lru_conv_w, lru_conv_b, lru_wa, lru_ba, lru_wx, lru_bx, lru_lambda, hgrn_lb_logits, hgrn_norm_g, ev_w_out, od_w_in, od_b_in, sc_conv_w, cf_conv_w, cf_conv_b, cf_ln_g, cf_ln_b, od_w_out, ffn_w_gate, ffn_w_up, ffn_w_down):
    B, S, D = x.shape
    depth = ln_mix_g.shape[0]
    xt = x.reshape(B * S, D)

    def rows(p):
        return p.reshape(p.shape[0], 1, p.shape[-1])

    ln_mix = rows(ln_mix_g)
    ln_ffn = rows(ln_ffn_g)
    ln_fin = ln_final_g.reshape(1, 1, D)
    ev_b = rows(ev_b_in)
    od_b = rows(od_b_in)
    w_gates = jnp.concatenate([lru_wa, lru_wx], axis=-1)
    n_even = lru_wa.shape[0]
    lru_ba2 = lru_ba.reshape(n_even, 1, -1)
    lru_bx2 = lru_bx.reshape(n_even, 1, -1)

    xg, ss = prenorm(xt, ln_mix, 0)
    for layer in range(depth):
        j = layer // 2
        if layer % 2 == 0:
            p = proj(xg, ss, ev_w_in, ev_b, j)
            ya = lru_mixer(p, B, lru_conv_w, rows(lru_conv_b), w_gates, lru_ba2, lru_bx2,
                           rows(lru_lambda), j)
            yb = hgrn_mixer(p, B, hgrn_lb_logits, rows(hgrn_norm_g), j)
            xt, xg, ss = out_proj(ya, yb, ev_w_out, j, xt, ln_ffn, layer)
        else:
            p = proj(xg, ss, od_w_in, od_b, j)
            yc, yd = odd_mixer(p, B, sc_conv_w, cf_conv_w, rows(cf_conv_b), rows(cf_ln_g),
                               rows(cf_ln_b), j)
            xt, xg, ss = out_proj(yc, yd, od_w_out, j, xt, ln_ffn, layer)
        a = ffn_up(xg, ss, ffn_w_gate, ffn_w_up, layer)
        if layer + 1 < depth:
            xt, xg, ss = ffn_down(a, ffn_w_down, layer, xt, ln_mix, layer + 1)
        else:
            xt = ffn_down(a, ffn_w_down, layer, xt, None, 0)
    out = rmsnorm(xt, ln_fin, 0, F32)
    return out.reshape(B, S, D)
```

```python
import functools

import jax
import jax.numpy as jnp
from jax import lax
from jax.experimental import pallas as pl
from jax.experimental.pallas import tpu as pltpu

F32 = jnp.float32
BF16 = jnp.bfloat16

EPS = 1e-6
F_FLOOR = 1e-30
LRU_C = 8.0
LRU_HEADS = 8
LRU_CONV = 4
HGRN_HEADS = 8
HEAD_DIM = 128
SC_CONV = 3
CF_CONV = 31
HGRN_SUB = 32
LOG2_E = 1.4426950408889634

SS_LANES = 8
V7X_SUBLANES = 8
V7X_VMEM_BYTES = 64 * 2**20
MIB = 2**20
KERNEL_VMEM_LIMIT = V7X_VMEM_BYTES - 8 * MIB

TILE_IN = (1024, 1024)
TILE_OUT = (512, 2048)
TILE_UP = (1024, 512)
TILE_DOWN = (512, 512)


def _params(*semantics):
    return pltpu.CompilerParams(dimension_semantics=semantics, vmem_limit_bytes=KERNEL_VMEM_LIMIT)


def _sigmoid(x):
    return jax.nn.sigmoid(x)


def _sumsq_lanes(x):
    return jnp.broadcast_to(jnp.sum(x * x, axis=-1, keepdims=True), (x.shape[0], SS_LANES))


def _row_scale(ss_ref, d_model):
    ms = jnp.sum(ss_ref[...], axis=0)[:, 0:1] * (1.0 / d_model)
    return lax.rsqrt(ms + EPS)


def _prenorm_body(x_ref, g_ref, xg_ref, ss_ref):
    x = x_ref[...]
    xg_ref[...] = (x * g_ref[...]).astype(xg_ref.dtype)
    ss_ref[...] = _sumsq_lanes(x)


def prenorm(x, g_stack, layer, tm=512):
    T, D = x.shape
    return pl.pallas_call(
        _prenorm_body,
        grid=(T // tm,),
        in_specs=[
            pl.BlockSpec((tm, D), lambda i: (i, 0)),
            pl.BlockSpec((None, 1, D), lambda i: (layer, 0, 0)),
        ],
        out_specs=[
            pl.BlockSpec((tm, D), lambda i: (i, 0)),
            pl.BlockSpec((None, tm, SS_LANES), lambda i: (0, i, 0)),
        ],
        out_shape=[jax.ShapeDtypeStruct((T, D), BF16),
                   jax.ShapeDtypeStruct((1, T, SS_LANES), F32)],
        compiler_params=_params("arbitrary"),
        name="prenorm",
    )(x, g_stack)


def _rmsnorm_body(x_ref, g_ref, o_ref):
    x = x_ref[...]
    ms = jnp.mean(x * x, axis=-1, keepdims=True)
    o_ref[...] = ((x * lax.rsqrt(ms + EPS)) * g_ref[...]).astype(o_ref.dtype)


def rmsnorm(x, g_stack, layer, out_dtype, tm=512):
    T, D = x.shape
    return pl.pallas_call(
        _rmsnorm_body,
        grid=(T // tm,),
        in_specs=[
            pl.BlockSpec((tm, D), lambda i: (i, 0)),
            pl.BlockSpec((None, 1, D), lambda i: (layer, 0, 0)),
        ],
        out_specs=pl.BlockSpec((tm, D), lambda i: (i, 0)),
        out_shape=jax.ShapeDtypeStruct((T, D), out_dtype),
        compiler_params=_params("arbitrary"),
        name="rmsnorm",
    )(x, g_stack)


def _proj_body(xg_ref, ss_ref, w_ref, b_ref, o_ref, wbf_ref, *, d_model):
    @pl.when(pl.program_id(1) == 0)
    def _():
        wbf_ref[...] = w_ref[...].astype(BF16)

    acc = jnp.dot(xg_ref[...], wbf_ref[...], preferred_element_type=F32)
    o_ref[...] = _row_scale(ss_ref, d_model) * acc + b_ref[...]


def proj(xg, ss, w_stack, b_stack, layer):
    tm, tn = TILE_IN
    T, K = xg.shape
    N = w_stack.shape[-1]
    parts = ss.shape[0]
    return pl.pallas_call(
        functools.partial(_proj_body, d_model=K),
        grid=(N // tn, T // tm),
        in_specs=[
            pl.BlockSpec((tm, K), lambda n, m: (m, 0)),
            pl.BlockSpec((parts, tm, SS_LANES), lambda n, m: (0, m, 0)),
            pl.BlockSpec((None, K, tn), lambda n, m: (layer, 0, n)),
            pl.BlockSpec((None, 1, tn), lambda n, m: (layer, 0, n)),
        ],
        out_specs=pl.BlockSpec((tm, tn), lambda n, m: (m, n)),
        out_shape=jax.ShapeDtypeStruct((T, N), F32),
        scratch_shapes=[pltpu.VMEM((K, tn), BF16)],
        compiler_params=_params("arbitrary", "arbitrary"),
        name="in_proj",
    )(xg, ss, w_stack, b_stack)


def _emit_residual(xn, gn_ref, o_ref, xg_ref, ss_ref):
    o_ref[...] = xn
    if gn_ref is not None:
        xg_ref[...] = (xn * gn_ref[...]).astype(xg_ref.dtype)
        ss_ref[...] = _sumsq_lanes(xn)


def _out_body(*refs, emit_norm):
    if emit_norm:
        ya_ref, yb_ref, wa_ref, wb_ref, x_ref, gn_ref, o_ref, xg_ref, ss_ref, wa_bf, wb_bf = refs
    else:
        ya_ref, yb_ref, wa_ref, wb_ref, x_ref, o_ref, wa_bf, wb_bf = refs
        gn_ref = xg_ref = ss_ref = None

    @pl.when(pl.program_id(1) == 0)
    def _():
        wa_bf[...] = wa_ref[...].astype(BF16)
        wb_bf[...] = wb_ref[...].astype(BF16)

    acc = jnp.dot(ya_ref[...], wa_bf[...], preferred_element_type=F32)
    acc = acc + jnp.dot(yb_ref[...], wb_bf[...], preferred_element_type=F32)
    _emit_residual(x_ref[...] + acc, gn_ref, o_ref, xg_ref, ss_ref)


def _residual_specs(T, N, tm, tn, next_gain, next_layer):
    x_spec = pl.BlockSpec((tm, tn), lambda n, m: (m, n))
    if next_gain is None:
        return [], [], x_spec, jax.ShapeDtypeStruct((T, N), F32)
    ins = [pl.BlockSpec((None, 1, tn), lambda n, m: (next_layer, 0, n))]
    outs = [x_spec, x_spec, pl.BlockSpec((None, tm, SS_LANES), lambda n, m: (n, m, 0))]
    shapes = [jax.ShapeDtypeStruct((T, N), F32), jax.ShapeDtypeStruct((T, N), BF16),
              jax.ShapeDtypeStruct((N // tn, T, SS_LANES), F32)]
    return ins, [next_gain], outs, shapes


def out_proj(ya, yb, w_stack, layer, x, next_gain, next_layer):
    tm, tn = TILE_OUT
    T, Kh = ya.shape
    N = w_stack.shape[-1]
    g_in, g_arg, outs, shapes = _residual_specs(T, N, tm, tn, next_gain, next_layer)
    return pl.pallas_call(
        functools.partial(_out_body, emit_norm=next_gain is not None),
        grid=(N // tn, T // tm),
        in_specs=[
            pl.BlockSpec((tm, Kh), lambda n, m: (m, 0)),
            pl.BlockSpec((tm, Kh), lambda n, m: (m, 0)),
            pl.BlockSpec((None, Kh, tn), lambda n, m: (layer, 0, n), pipeline_mode=pl.Buffered(1)),
            pl.BlockSpec((None, Kh, tn), lambda n, m: (layer, 1, n), pipeline_mode=pl.Buffered(1)),
            pl.BlockSpec((tm, tn), lambda n, m: (m, n)),
        ] + g_in,
        out_specs=outs,
        out_shape=shapes,
        scratch_shapes=[pltpu.VMEM((Kh, tn), BF16), pltpu.VMEM((Kh, tn), BF16)],
        compiler_params=_params("arbitrary", "arbitrary"),
        name="out_proj",
    )(ya, yb, w_stack, w_stack, x, *g_arg)


def _ffn_up_body(xg_ref, ss_ref, wg_ref, wu_ref, o_ref, wg_bf, wu_bf, *, d_model):
    @pl.when(pl.program_id(1) == 0)
    def _():
        wg_bf[...] = wg_ref[...].astype(BF16)
        wu_bf[...] = wu_ref[...].astype(BF16)

    h = xg_ref[...]
    rs = _row_scale(ss_ref, d_model)
    g = rs * jnp.dot(h, wg_bf[...], preferred_element_type=F32)
    u = rs * jnp.dot(h, wu_bf[...], preferred_element_type=F32)
    o_ref[...] = ((g * _sigmoid(g)) * u).astype(o_ref.dtype)


def ffn_up(xg, ss, wg_stack, wu_stack, layer):
    tm, tn = TILE_UP
    T, K = xg.shape
    F = wg_stack.shape[-1]
    parts = ss.shape[0]
    return pl.pallas_call(
        functools.partial(_ffn_up_body, d_model=K),
        grid=(F // tn, T // tm),
        in_specs=[
            pl.BlockSpec((tm, K), lambda n, m: (m, 0)),
            pl.BlockSpec((parts, tm, SS_LANES), lambda n, m: (0, m, 0)),
            pl.BlockSpec((None, K, tn), lambda n, m: (layer, 0, n)),
            pl.BlockSpec((None, K, tn), lambda n, m: (layer, 0, n)),
        ],
        out_specs=pl.BlockSpec((tm, tn), lambda n, m: (m, n)),
        out_shape=jax.ShapeDtypeStruct((T, F), BF16),
        scratch_shapes=[pltpu.VMEM((K, tn), BF16), pltpu.VMEM((K, tn), BF16)],
        compiler_params=_params("arbitrary", "arbitrary"),
        name="ffn_up",
    )(xg, ss, wg_stack, wu_stack)


def _ffn_down_body(*refs, emit_norm):
    if emit_norm:
        a_ref, w_ref, x_ref, gn_ref, o_ref, xg_ref, ss_ref, wbf_ref = refs
    else:
        a_ref, w_ref, x_ref, o_ref, wbf_ref = refs
        gn_ref = xg_ref = ss_ref = None

    @pl.when(pl.program_id(1) == 0)
    def _():
        wbf_ref[...] = w_ref[...].astype(BF16)

    acc = jnp.dot(a_ref[...], wbf_ref[...], preferred_element_type=F32)
    _emit_residual(x_ref[...] + acc, gn_ref, o_ref, xg_ref, ss_ref)


def ffn_down(a, w_stack, layer, x, next_gain, next_layer):
    tm, tn = TILE_DOWN
    T, F = a.shape
    N = w_stack.shape[-1]
    g_in, g_arg, outs, shapes = _residual_specs(T, N, tm, tn, next_gain, next_layer)
    return pl.pallas_call(
        functools.partial(_ffn_down_body, emit_norm=next_gain is not None),
        grid=(N // tn, T // tm),
        in_specs=[
            pl.BlockSpec((tm, F), lambda n, m: (m, 0)),
            pl.BlockSpec((None, F, tn), lambda n, m: (layer, 0, n)),
            pl.BlockSpec((tm, tn), lambda n, m: (m, n)),
        ] + g_in,
        out_specs=outs,
        out_shape=shapes,
        scratch_shapes=[pltpu.VMEM((F, tn), BF16)],
        compiler_params=_params("arbitrary", "arbitrary"),
        name="ffn_down",
    )(a, w_stack, x, *g_arg)


def _lru_body(xa_ref, gate_ref, cw_ref, cb_ref, wg_ref, ba_ref, bx_ref, lam_ref, y_ref,
              xe_ref, hc_ref, *, ts):
    i = pl.program_id(1)
    pad = V7X_SUBLANES
    width = xa_ref.shape[1]

    @pl.when(i == 0)
    def _():
        xe_ref[0:pad, :] = jnp.zeros((pad, width), F32)
        hc_ref[...] = jnp.zeros(hc_ref.shape, F32)

    xe_ref[pad:pad + ts, :] = xa_ref[...]
    xc = cw_ref[0:1, :] * xe_ref[pl.ds(pad - (LRU_CONV - 1), ts), :]
    for k in range(1, LRU_CONV):
        xc = xc + cw_ref[k:k + 1, :] * xe_ref[pl.ds(pad - (LRU_CONV - 1) + k, ts), :]
    xc = xc + cb_ref[...]
    xe_ref[0:pad, :] = xe_ref[ts:ts + pad, :]

    xcb = xc.astype(BF16)
    r_parts, i_parts = [], []
    for h in range(LRU_HEADS):
        gh = jnp.dot(xcb[:, h * HEAD_DIM:(h + 1) * HEAD_DIM], wg_ref[h].astype(BF16),
                     preferred_element_type=F32)
        r_parts.append(gh[:, :HEAD_DIM])
        i_parts.append(gh[:, HEAD_DIM:])
    r = _sigmoid(jnp.concatenate(r_parts, axis=1) + ba_ref[...])
    ig = _sigmoid(jnp.concatenate(i_parts, axis=1) + bx_ref[...])

    nl = -lam_ref[...]
    softplus = jnp.maximum(nl, 0.0) + jnp.log1p(jnp.exp(-jnp.abs(nl)))
    log_a = (-LRU_C * r) * softplus
    a = jnp.exp(log_a)
    mult = jnp.sqrt(jnp.maximum(jnp.tanh(-log_a) * (a * a + 1.0), 0.0))
    row = lax.broadcasted_iota(jnp.int32, (pad, width), 0)
    first = jnp.where(jnp.logical_and(i == 0, row == 0), 1.0, mult[:pad])
    mult = jnp.concatenate([first, mult[pad:]], axis=0)
    u = (mult * ig) * xc

    tiles = ts // pad
    a3 = a.reshape(tiles, pad, width)
    u3 = u.reshape(tiles, pad, width)
    subrow = lax.broadcasted_iota(jnp.int32, (tiles, pad, width), 1)
    d = 1
    while d < pad:
        keep = subrow >= d
        a_s = jnp.where(keep, pltpu.roll(a3, d, 1), 1.0)
        u_s = jnp.where(keep, pltpu.roll(u3, d, 1), 0.0)
        u3 = a3 * u_s + u3
        a3 = a3 * a_s
        d *= 2
    h_prev = hc_ref[0:1, :]
    h_tiles = []
    for t in range(tiles):
        h_tile = a3[t] * h_prev + u3[t]
        h_tiles.append(h_tile)
        h_prev = h_tile[pad - 1:pad, :]
    hc_ref[0:1, :] = h_prev
    hseq = jnp.concatenate(h_tiles, axis=0)

    g = gate_ref[...]
    c = 0.7978845608028654
    cdf = 0.5 * (1.0 + jnp.tanh(c * (g + 0.044715 * (g * g * g))))
    y_ref[...] = (hseq * (g * cdf)).astype(y_ref.dtype)


def lru_mixer(proj_out, batch, conv_w, conv_b, w_gates, ba, bx, lam, layer, ts=256):
    T = proj_out.shape[0]
    W = LRU_HEADS * HEAD_DIM
    nt = T // batch // ts
    vec = pl.BlockSpec((None, 1, W), lambda b, i: (layer, 0, 0))
    return pl.pallas_call(
        functools.partial(_lru_body, ts=ts),
        grid=(batch, nt),
        in_specs=[
            pl.BlockSpec((ts, W), lambda b, i: (b * nt + i, 0)),
            pl.BlockSpec((ts, W), lambda b, i: (b * nt + i, 1)),
            pl.BlockSpec((None, LRU_CONV, W), lambda b, i: (layer, 0, 0)),
            vec,
            pl.BlockSpec((None, LRU_HEADS, HEAD_DIM, 2 * HEAD_DIM), lambda b, i: (layer, 0, 0, 0)),
            vec, vec, vec,
        ],
        out_specs=pl.BlockSpec((ts, W), lambda b, i: (b * nt + i, 0)),
        out_shape=jax.ShapeDtypeStruct((T, W), BF16),
        scratch_shapes=[
            pltpu.VMEM((ts + 8, W), F32),
            pltpu.VMEM((8, W), F32),
        ],
        compiler_params=_params("arbitrary", "arbitrary"),
        name="rg_lru",
    )(proj_out, proj_out, conv_w, conv_b, w_gates, ba, bx, lam)


def _cumsum_rows(x):
    n, width = x.shape
    sub = V7X_SUBLANES
    x3 = x.reshape(n // sub, sub, width)
    subrow = lax.broadcasted_iota(jnp.int32, x3.shape, 1)
    d = 1
    while d < sub:
        x3 = x3 + jnp.where(subrow >= d, pltpu.roll(x3, d, 1), 0.0)
        d *= 2
    tiles = [x3[0]]
    for t in range(1, n // sub):
        tiles.append(x3[t] + tiles[-1][sub - 1:sub, :])
    return jnp.concatenate(tiles, axis=0)


def _hgrn_body(q_ref, f_ref, v_ref, g_ref, lbl_ref, ng_ref, y_ref, st_ref, rows_ref, *, ts, layer):
    @pl.when(pl.program_id(0) == 0)
    def _():
        st_ref[...] = jnp.zeros(st_ref.shape, F32)

    logits = lbl_ref[...]
    e = jnp.exp(logits - jnp.max(logits, axis=0, keepdims=True))
    sm = e / jnp.sum(e, axis=0, keepdims=True)
    lb = jnp.sum(sm[0:layer + 1, :], axis=0, keepdims=True) - sm[0:1, :]
    one_m_lb = 1.0 - lb
    ng = ng_ref[...]
    n = HGRN_SUB
    sub = V7X_SUBLANES
    batch, _, width = q_ref.shape
    subrow = lax.broadcasted_iota(jnp.int32, (sub, width), 0)
    nt_dims = (((1,), (1,)), ((), ()))
    tn_dims = (((0,), (0,)), ((), ()))

    t_idx = lax.broadcasted_iota(jnp.int32, (n, n), 0)
    s_idx = lax.broadcasted_iota(jnp.int32, (n, n), 1)
    diag_mask = t_idx == s_idx
    level_mask = {}
    h = 1
    while h < n:
        level_mask[h] = jnp.logical_and(t_idx // (2 * h) == s_idx // (2 * h),
                                        jnp.logical_and((t_idx // h) % 2 == 1, (s_idx // h) % 2 == 0))
        h *= 2

    def ref_rows(b, h):
        pieces = []
        for i in range(n // sub):
            base = i * sub
            if 2 * h >= sub:
                rho = (base // (2 * h)) * (2 * h) + h - 1
                pieces.append(jnp.broadcast_to(rows_ref[b, pl.ds(rho, 1), :], (sub, width)))
            else:
                piece = None
                for c in range(sub // (2 * h)):
                    rho = base + c * 2 * h + h - 1
                    rowv = jnp.broadcast_to(rows_ref[b, pl.ds(rho, 1), :], (sub, width))
                    piece = rowv if piece is None else jnp.where(subrow >= c * 2 * h, rowv, piece)
                pieces.append(piece)
        return jnp.concatenate(pieces, axis=0)

    def sub_block(b, j):
        r0 = pl.multiple_of(j * n, n)
        f = f_ref[b, pl.ds(r0, n), :]
        q = q_ref[b, pl.ds(r0, n), :]
        v = v_ref[b, pl.ds(r0, n), :]
        sig = _sigmoid(f)
        fg = jnp.maximum(lb + one_m_lb * sig, F_FLOOR)
        k = one_m_lb * (1.0 - sig)
        qf = q * _sigmoid(q)
        bc = _cumsum_rows(jnp.log(fg)) * LOG2_E
        rows_ref[b] = bc
        b_tot = bc[n - 1:n, :]
        qb = qf.astype(BF16)
        kb = k.astype(BF16)
        vb = v.astype(BF16)
        qd = qb * jnp.exp2(bc).astype(BF16)
        kd = kb * jnp.exp2(b_tot - bc).astype(BF16)
        e_tot = jnp.exp2(b_tot)
        q01 = jnp.concatenate([qb, qb * fg.astype(BF16)], axis=0)
        levels = []
        h = n // 2
        while h >= 2:
            br = ref_rows(b, h)
            dist = bc - br
            dec = jnp.exp2(jnp.minimum(dist, -dist)).astype(BF16)
            levels.append((h, qb * dec, kb * dec))
            h //= 2

        heads = [slice(hd * HEAD_DIM, (hd + 1) * HEAD_DIM) for hd in range(HGRN_HEADS)]
        ws, inters = [], []
        for hd, sl in enumerate(heads):
            r01 = lax.dot_general(q01[:, sl], kb[:, sl], nt_dims, preferred_element_type=F32)
            w = jnp.where(diag_mask, r01[:n], 0.0)
            w = jnp.where(level_mask[1], r01[n:], w)
            for h, ql, kl in levels:
                r = lax.dot_general(ql[:, sl], kl[:, sl], nt_dims, preferred_element_type=F32)
                w = jnp.where(level_mask[h], r, w)
            ws.append(w.astype(BF16))
            st = st_ref[b, hd]
            inters.append(lax.dot_general(qd[:, sl], st.astype(BF16), nt_dims,
                                          preferred_element_type=F32))
            d_st = lax.dot_general(vb[:, sl], kd[:, sl], tn_dims, preferred_element_type=F32)
            st_ref[b, hd] = st * e_tot[:, sl] + d_st
        outs = [inters[hd] + jnp.dot(ws[hd], vb[:, sl], preferred_element_type=F32)
                for hd, sl in enumerate(heads)]
        return jnp.concatenate(outs, axis=1)

    def finish(b, j, o_all):
        r0 = pl.multiple_of(j * n, n)
        g = g_ref[b, pl.ds(r0, n), :]
        outs = []
        for hd in range(HGRN_HEADS):
            o = o_all[:, hd * HEAD_DIM:(hd + 1) * HEAD_DIM]
            outs.append(o * lax.rsqrt(jnp.mean(o * o, axis=-1, keepdims=True) + EPS))
        o_n = jnp.concatenate(outs, axis=1) * ng
        y_ref[b, pl.ds(r0, n), :] = (o_n * (g * _sigmoid(g))).astype(y_ref.dtype)

    def step(j, o_prev):
        for b in range(batch):
            finish(b, j - 1, o_prev[b])
        return tuple(sub_block(b, j) for b in range(batch))

    n_sub = ts // n
    o_last = lax.fori_loop(1, n_sub, step, tuple(sub_block(b, 0) for b in range(batch)))
    for b in range(batch):
        finish(b, n_sub - 1, o_last[b])


def hgrn_mixer(proj_out, batch, lb_logits, norm_g, layer, ts=512):
    T = proj_out.shape[0]
    W = HGRN_HEADS * HEAD_DIM
    seq = T // batch
    n_layers = lb_logits.shape[0]
    p3 = proj_out.reshape(batch, seq, proj_out.shape[1])

    def col(section):
        return pl.BlockSpec((batch, ts, W), lambda i: (0, i, section))

    y = pl.pallas_call(
        functools.partial(_hgrn_body, ts=ts, layer=layer),
        grid=(seq // ts,),
        in_specs=[
            col(2), col(3), col(4), col(5),
            pl.BlockSpec((n_layers, W), lambda i: (0, 0)),
            pl.BlockSpec((None, 1, W), lambda i: (layer, 0, 0)),
        ],
        out_specs=pl.BlockSpec((batch, ts, W), lambda i: (0, i, 0)),
        out_shape=jax.ShapeDtypeStruct((batch, seq, W), BF16),
        scratch_shapes=[
            pltpu.VMEM((batch, HGRN_HEADS, HEAD_DIM, HEAD_DIM), F32),
            pltpu.VMEM((batch, HGRN_SUB, W), F32),
        ],
        compiler_params=_params("arbitrary"),
        name="hgrn2",
    )(p3, p3, p3, p3, lb_logits, norm_g)
    return y.reshape(T, W)


ODD_PAD_P = V7X_SUBLANES
ODD_PAD_G = 4 * V7X_SUBLANES


def _odd_reset(pe_ref, ge_ref):
    pe_ref[0:ODD_PAD_P, :] = jnp.zeros((ODD_PAD_P, pe_ref.shape[1]), F32)
    ge_ref[0:ODD_PAD_G, :] = jnp.zeros((ODD_PAD_G, ge_ref.shape[1]), F32)


def _odd_tile(sb_ref, sc_ref, sv_ref, cu_ref, cg_ref, scw_ref, cfw_ref, cfb_ref, lng_ref, lnb_ref,
              pe_ref, ge_ref, d_ref, store, *, ts, rows, conv_rows, conv_lanes):
    width = sb_ref.shape[1]
    sub = V7X_SUBLANES
    pad_p = ODD_PAD_P
    pad_g = ODD_PAD_G

    pe_ref[pad_p:pad_p + ts, :] = sc_ref[...] * sv_ref[...]
    ge_ref[pad_g:pad_g + ts, :] = cu_ref[...] * _sigmoid(cg_ref[...])

    phases = {}
    for k in range(CF_CONV):
        off = pad_g - (CF_CONV - 1) + k
        phases.setdefault(off % sub, []).append((k, off - off % sub))
    for r0 in range(0, ts, conv_rows):
        for c0 in range(0, width, conv_lanes):
            cols = slice(c0, c0 + conv_lanes)
            acc = None
            for p, taps in sorted(phases.items()):
                nrows = conv_rows if p == 0 else conv_rows + sub
                z = None
                for k, q in taps:
                    term = cfw_ref[k:k + 1, cols] * ge_ref[r0 + q:r0 + q + nrows, cols]
                    z = term if z is None else z + term
                part = z if p == 0 else z[p:p + conv_rows]
                acc = part if acc is None else acc + part
            d_ref[r0:r0 + conv_rows, cols] = acc + cfb_ref[:, cols]

    for r0 in range(0, ts, rows):
        base = pad_p - (SC_CONV - 1) + r0
        acc = scw_ref[0:1, :] * pe_ref[base:base + rows, :]
        for k in range(1, SC_CONV):
            acc = acc + scw_ref[k:k + 1, :] * pe_ref[base + k:base + k + rows, :]
        y_c = sb_ref[r0:r0 + rows, :] * acc

        d = d_ref[r0:r0 + rows, :]
        mu = jnp.mean(d, axis=-1, keepdims=True)
        dc = d - mu
        var = jnp.mean(dc * dc, axis=-1, keepdims=True)
        z = (dc * lax.rsqrt(var + EPS)) * lng_ref[...] + lnb_ref[...]
        store(r0, y_c, z * _sigmoid(z))

    pe_ref[0:pad_p, :] = pe_ref[ts:ts + pad_p, :]
    ge_ref[0:pad_g, :] = ge_ref[ts:ts + pad_g, :]


def _odd_body(sb_ref, sc_ref, sv_ref, cu_ref, cg_ref, scw_ref, cfw_ref, cfb_ref, lng_ref, lnb_ref,
              yc_ref, yd_ref, pe_ref, ge_ref, d_ref, *, ts, rows, conv_rows, conv_lanes):
    @pl.when(pl.program_id(1) == 0)
    def _():
        _odd_reset(pe_ref, ge_ref)

    def store(r0, y_c, y_d):
        yc_ref[r0:r0 + rows, :] = y_c.astype(yc_ref.dtype)
        yd_ref[r0:r0 + rows, :] = y_d.astype(yd_ref.dtype)

    _odd_tile(sb_ref, sc_ref, sv_ref, cu_ref, cg_ref, scw_ref, cfw_ref, cfb_ref, lng_ref, lnb_ref,
              pe_ref, ge_ref, d_ref, store, ts=ts, rows=rows, conv_rows=conv_rows,
              conv_lanes=conv_lanes)


def odd_mixer(proj_out, batch, sc_w, cf_w, cf_b, ln_g, ln_b, layer, ts=256, rows=32,
              conv_rows=64, conv_lanes=256):
    T = proj_out.shape[0]
    W = proj_out.shape[1] // 5
    nt = T // batch // ts

    def col(section):
        return pl.BlockSpec((ts, W), lambda b, i: (b * nt + i, section))

    vec = pl.BlockSpec((None, 1, W), lambda b, i: (layer, 0, 0))
    out = pl.BlockSpec((ts, W), lambda b, i: (b * nt + i, 0))
    return pl.pallas_call(
        functools.partial(_odd_body, ts=ts, rows=rows, conv_rows=conv_rows, conv_lanes=conv_lanes),
        grid=(batch, nt),
        in_specs=[
            col(0), col(1), col(2), col(3), col(4),
            pl.BlockSpec((None, SC_CONV, W), lambda b, i: (layer, 0, 0)),
            pl.BlockSpec((None, CF_CONV, W), lambda b, i: (layer, 0, 0)),
            vec, vec, vec,
        ],
        out_specs=[out, out],
        out_shape=[jax.ShapeDtypeStruct((T, W), BF16), jax.ShapeDtypeStruct((T, W), BF16)],
        scratch_shapes=[
            pltpu.VMEM((ts + 8, W), F32),
            pltpu.VMEM((ts + 32, W), F32),
            pltpu.VMEM((ts, W), F32),
        ],
        compiler_params=_params("arbitrary", "arbitrary"),
        name="odd_mixer",
    )(proj_out, proj_out, proj_out, proj_out, proj_out, sc_w, cf_w, cf_b, ln_g, ln_b)


def kernel(x, ln_mix_g, ln_ffn_g, ln_final_g, ev_w_in, ev_b_in, lru_conv_w, lru_conv_b, lru_wa, lru_ba, lru_wx, lru_bx, lru_lambda, hgrn_lb_logits, hgrn_norm_g, ev_w_out, od_w_in, od_b_in, sc_conv_w, cf_conv_w, cf_conv_b, cf_ln_g, cf_ln_b, od_w_out, ffn_w_gate, ffn_w_up, ffn_w_down):
    B, S, D = x.shape
    depth = ln_mix_g.shape[0]
    xt = x.reshape(B * S, D)

    def rows(p):
        return p.reshape(p.shape[0], 1, p.shape[-1])

    ln_mix = rows(ln_mix_g)
    ln_ffn = rows(ln_ffn_g)
    ln_fin = ln_final_g.reshape(1, 1, D)
    ev_b = rows(ev_b_in)
    od_b = rows(od_b_in)
    w_gates = jnp.concatenate([lru_wa, lru_wx], axis=-1)
    n_even = lru_wa.shape[0]
    lru_ba2 = lru_ba.reshape(n_even, 1, -1)
    lru_bx2 = lru_bx.reshape(n_even, 1, -1)

    xg, ss = prenorm(xt, ln_mix, 0)
    for layer in range(depth):
        j = layer // 2
        if layer % 2 == 0:
            p = proj(xg, ss, ev_w_in, ev_b, j)
            ya = lru_mixer(p, B, lru_conv_w, rows(lru_conv_b), w_gates, lru_ba2, lru_bx2,
                           rows(lru_lambda), j)
            yb = hgrn_mixer(p, B, hgrn_lb_logits, rows(hgrn_norm_g), j)
            xt, xg, ss = out_proj(ya, yb, ev_w_out, j, xt, ln_ffn, layer)
        else:
            p = proj(xg, ss, od_w_in, od_b, j)
            yc, yd = odd_mixer(p, B, sc_conv_w, cf_conv_w, rows(cf_conv_b), rows(cf_ln_g),
                               rows(cf_ln_b), j)
            xt, xg, ss = out_proj(yc, yd, od_w_out, j, xt, ln_ffn, layer)
        a = ffn_up(xg, ss, ffn_w_gate, ffn_w_up, layer)
        if layer + 1 < depth:
            xt, xg, ss = ffn_down(a, ffn_w_down, layer, xt, ln_mix, layer + 1)
        else:
            xt = ffn_down(a, ffn_w_down, layer, xt, None, 0)
    out = rmsnorm(xt, ln_fin, 0, F32)
    return out.reshape(B, S, D)
```

```python
import functools

import jax
import jax.numpy as jnp
from jax import lax
from jax.experimental import pallas as pl
from jax.experimental.pallas import tpu as pltpu

F32 = jnp.float32
BF16 = jnp.bfloat16

EPS = 1e-6
F_FLOOR = 1e-30
LRU_C = 8.0
LRU_HEADS = 8
LRU_CONV = 4
HGRN_HEADS = 8
HEAD_DIM = 128
SC_CONV = 3
CF_CONV = 31
HGRN_SUB = 32
LOG2_E = 1.4426950408889634

SS_LANES = 8
V7X_SUBLANES = 8
V7X_VMEM_BYTES = 64 * 2**20
MIB = 2**20
KERNEL_VMEM_LIMIT = V7X_VMEM_BYTES - 8 * MIB

TILE_IN = (1024, 1024)
TILE_OUT = (512, 2048)
TILE_UP = (1024, 512)
TILE_DOWN = (512, 512)


def _params(*semantics):
    return pltpu.CompilerParams(dimension_semantics=semantics, vmem_limit_bytes=KERNEL_VMEM_LIMIT)


def _sigmoid(x):
    return jax.nn.sigmoid(x)


def _sumsq_lanes(x):
    return jnp.broadcast_to(jnp.sum(x * x, axis=-1, keepdims=True), (x.shape[0], SS_LANES))


def _row_scale(ss_ref, d_model):
    ms = jnp.sum(ss_ref[...], axis=0)[:, 0:1] * (1.0 / d_model)
    return lax.rsqrt(ms + EPS)


def _prenorm_body(x_ref, g_ref, xg_ref, ss_ref):
    x = x_ref[...]
    xg_ref[...] = (x * g_ref[...]).astype(xg_ref.dtype)
    ss_ref[...] = _sumsq_lanes(x)


def prenorm(x, g_stack, layer, tm=512):
    T, D = x.shape
    return pl.pallas_call(
        _prenorm_body,
        grid=(T // tm,),
        in_specs=[
            pl.BlockSpec((tm, D), lambda i: (i, 0)),
            pl.BlockSpec((None, 1, D), lambda i: (layer, 0, 0)),
        ],
        out_specs=[
            pl.BlockSpec((tm, D), lambda i: (i, 0)),
            pl.BlockSpec((None, tm, SS_LANES), lambda i: (0, i, 0)),
        ],
        out_shape=[jax.ShapeDtypeStruct((T, D), BF16),
                   jax.ShapeDtypeStruct((1, T, SS_LANES), F32)],
        compiler_params=_params("arbitrary"),
        name="prenorm",
    )(x, g_stack)


def _rmsnorm_body(x_ref, g_ref, o_ref):
    x = x_ref[...]
    ms = jnp.mean(x * x, axis=-1, keepdims=True)
    o_ref[...] = ((x * lax.rsqrt(ms + EPS)) * g_ref[...]).astype(o_ref.dtype)


def rmsnorm(x, g_stack, layer, out_dtype, tm=512):
    T, D = x.shape
    return pl.pallas_call(
        _rmsnorm_body,
        grid=(T // tm,),
        in_specs=[
            pl.BlockSpec((tm, D), lambda i: (i, 0)),
            pl.BlockSpec((None, 1, D), lambda i: (layer, 0, 0)),
        ],
        out_specs=pl.BlockSpec((tm, D), lambda i: (i, 0)),
        out_shape=jax.ShapeDtypeStruct((T, D), out_dtype),
        compiler_params=_params("arbitrary"),
        name="rmsnorm",
    )(x, g_stack)


def _proj_body(xg_ref, ss_ref, w_ref, b_ref, o_ref, wbf_ref, *, d_model):
    @pl.when(pl.program_id(1) == 0)
    def _():
        wbf_ref[...] = w_ref[...].astype(BF16)

    acc = jnp.dot(xg_ref[...], wbf_ref[...], preferred_element_type=F32)
    o_ref[...] = _row_scale(ss_ref, d_model) * acc + b_ref[...]


def proj(xg, ss, w_stack, b_stack, layer):
    tm, tn = TILE_IN
    T, K = xg.shape
    N = w_stack.shape[-1]
    parts = ss.shape[0]
    return pl.pallas_call(
        functools.partial(_proj_body, d_model=K),
        grid=(N // tn, T // tm),
        in_specs=[
            pl.BlockSpec((tm, K), lambda n, m: (m, 0)),
            pl.BlockSpec((parts, tm, SS_LANES), lambda n, m: (0, m, 0)),
            pl.BlockSpec((None, K, tn), lambda n, m: (layer, 0, n)),
            pl.BlockSpec((None, 1, tn), lambda n, m: (layer, 0, n)),
        ],
        out_specs=pl.BlockSpec((tm, tn), lambda n, m: (m, n)),
        out_shape=jax.ShapeDtypeStruct((T, N), F32),
        scratch_shapes=[pltpu.VMEM((K, tn), BF16)],
        compiler_params=_params("arbitrary", "arbitrary"),
        name="in_proj",
    )(xg, ss, w_stack, b_stack)


def _emit_residual(xn, gn_ref, o_ref, xg_ref, ss_ref):
    o_ref[...] = xn
    if gn_ref is not None:
        xg_ref[...] = (xn * gn_ref[...]).astype(xg_ref.dtype)
        ss_ref[...] = _sumsq_lanes(xn)


def _out_body(*refs, emit_norm):
    if emit_norm:
        ya_ref, yb_ref, wa_ref, wb_ref, x_ref, gn_ref, o_ref, xg_ref, ss_ref, wa_bf, wb_bf = refs
    else:
        ya_ref, yb_ref, wa_ref, wb_ref, x_ref, o_ref, wa_bf, wb_bf = refs
        gn_ref = xg_ref = ss_ref = None

    @pl.when(pl.program_id(1) == 0)
    def _():
        wa_bf[...] = wa_ref[...].astype(BF16)
        wb_bf[...] = wb_ref[...].astype(BF16)

    acc = jnp.dot(ya_ref[...], wa_bf[...], preferred_element_type=F32)
    acc = acc + jnp.dot(yb_ref[...], wb_bf[...], preferred_element_type=F32)
    _emit_residual(x_ref[...] + acc, gn_ref, o_ref, xg_ref, ss_ref)


def _residual_specs(T, N, tm, tn, next_gain, next_layer):
    x_spec = pl.BlockSpec((tm, tn), lambda n, m: (m, n))
    if next_gain is None:
        return [], [], x_spec, jax.ShapeDtypeStruct((T, N), F32)
    ins = [pl.BlockSpec((None, 1, tn), lambda n, m: (next_layer, 0, n))]
    outs = [x_spec, x_spec, pl.BlockSpec((None, tm, SS_LANES), lambda n, m: (n, m, 0))]
    shapes = [jax.ShapeDtypeStruct((T, N), F32), jax.ShapeDtypeStruct((T, N), BF16),
              jax.ShapeDtypeStruct((N // tn, T, SS_LANES), F32)]
    return ins, [next_gain], outs, shapes


def out_proj(ya, yb, w_stack, layer, x, next_gain, next_layer):
    tm, tn = TILE_OUT
    T, Kh = ya.shape
    N = w_stack.shape[-1]
    g_in, g_arg, outs, shapes = _residual_specs(T, N, tm, tn, next_gain, next_layer)
    return pl.pallas_call(
        functools.partial(_out_body, emit_norm=next_gain is not None),
        grid=(N // tn, T // tm),
        in_specs=[
            pl.BlockSpec((tm, Kh), lambda n, m: (m, 0)),
            pl.BlockSpec((tm, Kh), lambda n, m: (m, 0)),
            pl.BlockSpec((None, Kh, tn), lambda n, m: (layer, 0, n), pipeline_mode=pl.Buffered(1)),
            pl.BlockSpec((None, Kh, tn), lambda n, m: (layer, 1, n), pipeline_mode=pl.Buffered(1)),
            pl.BlockSpec((tm, tn), lambda n, m: (m, n)),
        ] + g_in,
        out_specs=outs,
        out_shape=shapes,
        scratch_shapes=[pltpu.VMEM((Kh, tn), BF16), pltpu.VMEM((Kh, tn), BF16)],
        compiler_params=_params("arbitrary", "arbitrary"),
        name="out_proj",
    )(ya, yb, w_stack, w_stack, x, *g_arg)


def _ffn_up_body(xg_ref, ss_ref, wg_ref, wu_ref, o_ref, wg_bf, wu_bf, *, d_model):
    @pl.when(pl.program_id(1) == 0)
    def _():
        wg_bf[...] = wg_ref[...].astype(BF16)
        wu_bf[...] = wu_ref[...].astype(BF16)

    h = xg_ref[...]
    rs = _row_scale(ss_ref, d_model)
    g = rs * jnp.dot(h, wg_bf[...], preferred_element_type=F32)
    u = rs * jnp.dot(h, wu_bf[...], preferred_element_type=F32)
    o_ref[...] = ((g * _sigmoid(g)) * u).astype(o_ref.dtype)


def ffn_up(xg, ss, wg_stack, wu_stack, layer):
    tm, tn = TILE_UP
    T, K = xg.shape
    F = wg_stack.shape[-1]
    parts = ss.shape[0]
    return pl.pallas_call(
        functools.partial(_ffn_up_body, d_model=K),
        grid=(F // tn, T // tm),
        in_specs=[
            pl.BlockSpec((tm, K), lambda n, m: (m, 0)),
            pl.BlockSpec((parts, tm, SS_LANES), lambda n, m: (0, m, 0)),
            pl.BlockSpec((None, K, tn), lambda n, m: (layer, 0, n)),
            pl.BlockSpec((None, K, tn), lambda n, m: (layer, 0, n)),
        ],
        out_specs=pl.BlockSpec((tm, tn), lambda n, m: (m, n)),
        out_shape=jax.ShapeDtypeStruct((T, F), BF16),
        scratch_shapes=[pltpu.VMEM((K, tn), BF16), pltpu.VMEM((K, tn), BF16)],
        compiler_params=_params("arbitrary", "arbitrary"),
        name="ffn_up",
    )(xg, ss, wg_stack, wu_stack)


def _ffn_down_body(*refs, emit_norm):
    if emit_norm:
        a_ref, w_ref, x_ref, gn_ref, o_ref, xg_ref, ss_ref, wbf_ref = refs
    else:
        a_ref, w_ref, x_ref, o_ref, wbf_ref = refs
        gn_ref = xg_ref = ss_ref = None

    @pl.when(pl.program_id(1) == 0)
    def _():
        wbf_ref[...] = w_ref[...].astype(BF16)

    acc = jnp.dot(a_ref[...], wbf_ref[...], preferred_element_type=F32)
    _emit_residual(x_ref[...] + acc, gn_ref, o_ref, xg_ref, ss_ref)


def ffn_down(a, w_stack, layer, x, next_gain, next_layer):
    tm, tn = TILE_DOWN
    T, F = a.shape
    N = w_stack.shape[-1]
    g_in, g_arg, outs, shapes = _residual_specs(T, N, tm, tn, next_gain, next_layer)
    return pl.pallas_call(
        functools.partial(_ffn_down_body, emit_norm=next_gain is not None),
        grid=(N // tn, T // tm),
        in_specs=[
            pl.BlockSpec((tm, F), lambda n, m: (m, 0)),
            pl.BlockSpec((None, F, tn), lambda n, m: (layer, 0, n)),
            pl.BlockSpec((tm, tn), lambda n, m: (m, n)),
        ] + g_in,
        out_specs=outs,
        out_shape=shapes,
        scratch_shapes=[pltpu.VMEM((F, tn), BF16)],
        compiler_params=_params("arbitrary", "arbitrary"),
        name="ffn_down",
    )(a, w_stack, x, *g_arg)


def _lru_body(xa_ref, gate_ref, cw_ref, cb_ref, wg_ref, ba_ref, bx_ref, lam_ref, y_ref,
              xe_ref, hc_ref, *, ts):
    i = pl.program_id(1)
    pad = V7X_SUBLANES
    width = xa_ref.shape[1]

    @pl.when(i == 0)
    def _():
        xe_ref[0:pad, :] = jnp.zeros((pad, width), F32)
        hc_ref[...] = jnp.zeros(hc_ref.shape, F32)

    xe_ref[pad:pad + ts, :] = xa_ref[...]
    xc = cw_ref[0:1, :] * xe_ref[pl.ds(pad - (LRU_CONV - 1), ts), :]
    for k in range(1, LRU_CONV):
        xc = xc + cw_ref[k:k + 1, :] * xe_ref[pl.ds(pad - (LRU_CONV - 1) + k, ts), :]
    xc = xc + cb_ref[...]
    xe_ref[0:pad, :] = xe_ref[ts:ts + pad, :]

    xcb = xc.astype(BF16)
    r_parts, i_parts = [], []
    for h in range(LRU_HEADS):
        gh = jnp.dot(xcb[:, h * HEAD_DIM:(h + 1) * HEAD_DIM], wg_ref[h].astype(BF16),
                     preferred_element_type=F32)
        r_parts.append(gh[:, :HEAD_DIM])
        i_parts.append(gh[:, HEAD_DIM:])
    r = _sigmoid(jnp.concatenate(r_parts, axis=1) + ba_ref[...])
    ig = _sigmoid(jnp.concatenate(i_parts, axis=1) + bx_ref[...])

    nl = -lam_ref[...]
    softplus = jnp.maximum(nl, 0.0) + jnp.log1p(jnp.exp(-jnp.abs(nl)))
    log_a = (-LRU_C * r) * softplus
    a = jnp.exp(log_a)
    mult = jnp.sqrt(jnp.maximum(jnp.tanh(-log_a) * (a * a + 1.0), 0.0))
    row = lax.broadcasted_iota(jnp.int32, (pad, width), 0)
    first = jnp.where(jnp.logical_and(i == 0, row == 0), 1.0, mult[:pad])
    mult = jnp.concatenate([first, mult[pad:]], axis=0)
    u = (mult * ig) * xc

    tiles = ts // pad
    a3 = a.reshape(tiles, pad, width)
    u3 = u.reshape(tiles, pad, width)
    subrow = lax.broadcasted_iota(jnp.int32, (tiles, pad, width), 1)
    d = 1
    while d < pad:
        keep = subrow >= d
        a_s = jnp.where(keep, pltpu.roll(a3, d, 1), 1.0)
        u_s = jnp.where(keep, pltpu.roll(u3, d, 1), 0.0)
        u3 = a3 * u_s + u3
        a3 = a3 * a_s
        d *= 2
    h_prev = hc_ref[0:1, :]
    h_tiles = []
    for t in range(tiles):
        h_tile = a3[t] * h_prev + u3[t]
        h_tiles.append(h_tile)
        h_prev = h_tile[pad - 1:pad, :]
    hc_ref[0:1, :] = h_prev
    hseq = jnp.concatenate(h_tiles, axis=0)

    g = gate_ref[...]
    c = 0.7978845608028654
    cdf = 0.5 * (1.0 + jnp.tanh(c * (g + 0.044715 * (g * g * g))))
    y_ref[...] = (hseq * (g * cdf)).astype(y_ref.dtype)


def lru_mixer(proj_out, batch, conv_w, conv_b, w_gates, ba, bx, lam, layer, ts=256):
    T = proj_out.shape[0]
    W = LRU_HEADS * HEAD_DIM
    nt = T // batch // ts
    vec = pl.BlockSpec((None, 1, W), lambda b, i: (layer, 0, 0))
    return pl.pallas_call(
        functools.partial(_lru_body, ts=ts),
        grid=(batch, nt),
        in_specs=[
            pl.BlockSpec((ts, W), lambda b, i: (b * nt + i, 0)),
            pl.BlockSpec((ts, W), lambda b, i: (b * nt + i, 1)),
            pl.BlockSpec((None, LRU_CONV, W), lambda b, i: (layer, 0, 0)),
            vec,
            pl.BlockSpec((None, LRU_HEADS, HEAD_DIM, 2 * HEAD_DIM), lambda b, i: (layer, 0, 0, 0)),
            vec, vec, vec,
        ],
        out_specs=pl.BlockSpec((ts, W), lambda b, i: (b * nt + i, 0)),
        out_shape=jax.ShapeDtypeStruct((T, W), BF16),
        scratch_shapes=[
            pltpu.VMEM((ts + 8, W), F32),
            pltpu.VMEM((8, W), F32),
        ],
        compiler_params=_params("arbitrary", "arbitrary"),
        name="rg_lru",
    )(proj_out, proj_out, conv_w, conv_b, w_gates, ba, bx, lam)


def _cumsum_rows(x):
    n, width = x.shape
    sub = V7X_SUBLANES
    x3 = x.reshape(n // sub, sub, width)
    subrow = lax.broadcasted_iota(jnp.int32, x3.shape, 1)
    d = 1
    while d < sub:
        x3 = x3 + jnp.where(subrow >= d, pltpu.roll(x3, d, 1), 0.0)
        d *= 2
    tiles = [x3[0]]
    for t in range(1, n // sub):
        tiles.append(x3[t] + tiles[-1][sub - 1:sub, :])
    return jnp.concatenate(tiles, axis=0)


def _hgrn_body(q_ref, f_ref, v_ref, g_ref, lbl_ref, ng_ref, y_ref, st_ref, rows_ref, *, ts, layer):
    @pl.when(pl.program_id(0) == 0)
    def _():
        st_ref[...] = jnp.zeros(st_ref.shape, F32)

    logits = lbl_ref[...]
    e = jnp.exp(logits - jnp.max(logits, axis=0, keepdims=True))
    sm = e / jnp.sum(e, axis=0, keepdims=True)
    lb = jnp.sum(sm[0:layer + 1, :], axis=0, keepdims=True) - sm[0:1, :]
    one_m_lb = 1.0 - lb
    ng = ng_ref[...]
    n = HGRN_SUB
    sub = V7X_SUBLANES
    batch, _, width = q_ref.shape
    subrow = lax.broadcasted_iota(jnp.int32, (sub, width), 0)
    nt_dims = (((1,), (1,)), ((), ()))
    tn_dims = (((0,), (0,)), ((), ()))

    t_idx = lax.broadcasted_iota(jnp.int32, (n, n), 0)
    s_idx = lax.broadcasted_iota(jnp.int32, (n, n), 1)
    diag_mask = t_idx == s_idx
    level_mask = {}
    h = 1
    while h < n:
        level_mask[h] = jnp.logical_and(t_idx // (2 * h) == s_idx // (2 * h),
                                        jnp.logical_and((t_idx // h) % 2 == 1, (s_idx // h) % 2 == 0))
        h *= 2

    def ref_rows(b, h):
        pieces = []
        for i in range(n // sub):
            base = i * sub
            if 2 * h >= sub:
                rho = (base // (2 * h)) * (2 * h) + h - 1
                pieces.append(jnp.broadcast_to(rows_ref[b, pl.ds(rho, 1), :], (sub, width)))
            else:
                piece = None
                for c in range(sub // (2 * h)):
                    rho = base + c * 2 * h + h - 1
                    rowv = jnp.broadcast_to(rows_ref[b, pl.ds(rho, 1), :], (sub, width))
                    piece = rowv if piece is None else jnp.where(subrow >= c * 2 * h, rowv, piece)
                pieces.append(piece)
        return jnp.concatenate(pieces, axis=0)

    def sub_block(b, j):
        r0 = pl.multiple_of(j * n, n)
        f = f_ref[b, pl.ds(r0, n), :]
        q = q_ref[b, pl.ds(r0, n), :]
        v = v_ref[b, pl.ds(r0, n), :]
        sig = _sigmoid(f)
        fg = jnp.maximum(lb + one_m_lb * sig, F_FLOOR)
        k = one_m_lb * (1.0 - sig)
        qf = q * _sigmoid(q)
        bc = _cumsum_rows(jnp.log(fg)) * LOG2_E
        rows_ref[b] = bc
        b_tot = bc[n - 1:n, :]
        qb = qf.astype(BF16)
        kb = k.astype(BF16)
        vb = v.astype(BF16)
        qd = qb * jnp.exp2(bc).astype(BF16)
        kd = kb * jnp.exp2(b_tot - bc).astype(BF16)
        e_tot = jnp.exp2(b_tot)
        q01 = jnp.concatenate([qb, qb * fg.astype(BF16)], axis=0)
        levels = []
        h = n // 2
        while h >= 2:
            br = ref_rows(b, h)
            dist = bc - br
            dec = jnp.exp2(jnp.minimum(dist, -dist)).astype(BF16)
            levels.append((h, qb * dec, kb * dec))
            h //= 2

        heads = [slice(hd * HEAD_DIM, (hd + 1) * HEAD_DIM) for hd in range(HGRN_HEADS)]
        ws, inters = [], []
        for hd, sl in enumerate(heads):
            r01 = lax.dot_general(q01[:, sl], kb[:, sl], nt_dims, preferred_element_type=F32)
            w = jnp.where(diag_mask, r01[:n], 0.0)
            w = jnp.where(level_mask[1], r01[n:], w)
            for h, ql, kl in levels:
                r = lax.dot_general(ql[:, sl], kl[:, sl], nt_dims, preferred_element_type=F32)
                w = jnp.where(level_mask[h], r, w)
            ws.append(w.astype(BF16))
            st = st_ref[b, hd]
            inters.append(lax.dot_general(qd[:, sl], st.astype(BF16), nt_dims,
                                          preferred_element_type=F32))
            d_st = lax.dot_general(vb[:, sl], kd[:, sl], tn_dims, preferred_element_type=F32)
            st_ref[b, hd] = st * e_tot[:, sl] + d_st
        outs = [inters[hd] + jnp.dot(ws[hd], vb[:, sl], preferred_element_type=F32)
                for hd, sl in enumerate(heads)]
        return jnp.concatenate(outs, axis=1)

    def finish(b, j, o_all):
        r0 = pl.multiple_of(j * n, n)
        g = g_ref[b, pl.ds(r0, n), :]
        outs = []
        for hd in range(HGRN_HEADS):
            o = o_all[:, hd * HEAD_DIM:(hd + 1) * HEAD_DIM]
            outs.append(o * lax.rsqrt(jnp.mean(o * o, axis=-1, keepdims=True) + EPS))
        o_n = jnp.concatenate(outs, axis=1) * ng
        y_ref[b, pl.ds(r0, n), :] = (o_n * (g * _sigmoid(g))).astype(y_ref.dtype)

    def step(j, o_prev):
        for b in range(batch):
            finish(b, j - 1, o_prev[b])
        return tuple(sub_block(b, j) for b in range(batch))

    n_sub = ts // n
    o_last = lax.fori_loop(1, n_sub, step, tuple(sub_block(b, 0) for b in range(batch)))
    for b in range(batch):
        finish(b, n_sub - 1, o_last[b])


def hgrn_mixer(proj_out, batch, lb_logits, norm_g, layer, ts=512):
    T = proj_out.shape[0]
    W = HGRN_HEADS * HEAD_DIM
    seq = T // batch
    n_layers = lb_logits.shape[0]
    p3 = proj_out.reshape(batch, seq, proj_out.shape[1])

    def col(section):
        return pl.BlockSpec((batch, ts, W), lambda i: (0, i, section))

    y = pl.pallas_call(
        functools.partial(_hgrn_body, ts=ts, layer=layer),
        grid=(seq // ts,),
        in_specs=[
            col(2), col(3), col(4), col(5),
            pl.BlockSpec((n_layers, W), lambda i: (0, 0)),
            pl.BlockSpec((None, 1, W), lambda i: (layer, 0, 0)),
        ],
        out_specs=pl.BlockSpec((batch, ts, W), lambda i: (0, i, 0)),
        out_shape=jax.ShapeDtypeStruct((batch, seq, W), BF16),
        scratch_shapes=[
            pltpu.VMEM((batch, HGRN_HEADS, HEAD_DIM, HEAD_DIM), F32),
            pltpu.VMEM((batch, HGRN_SUB, W), F32),
        ],
        compiler_params=_params("arbitrary"),
        name="hgrn2",
    )(p3, p3, p3, p3, lb_logits, norm_g)
    return y.reshape(T, W)


ODD_PAD_P = V7X_SUBLANES
ODD_PAD_G = 4 * V7X_SUBLANES


def _odd_reset(pe_ref, ge_ref):
    pe_ref[0:ODD_PAD_P, :] = jnp.zeros((ODD_PAD_P, pe_ref.shape[1]), F32)
    ge_ref[0:ODD_PAD_G, :] = jnp.zeros((ODD_PAD_G, ge_ref.shape[1]), F32)


def _odd_tile(sb_ref, sc_ref, sv_ref, cu_ref, cg_ref, scw_ref, cfw_ref, cfb_ref, lng_ref, lnb_ref,
              pe_ref, ge_ref, d_ref, store, *, ts, rows, conv_lanes):
    width = sb_ref.shape[1]
    sub = V7X_SUBLANES
    pad_p = ODD_PAD_P
    pad_g = ODD_PAD_G

    pe_ref[pad_p:pad_p + ts, :] = sc_ref[...] * sv_ref[...]
    ge_ref[pad_g:pad_g + ts, :] = cu_ref[...] * _sigmoid(cg_ref[...])

    phases = {}
    for k in range(CF_CONV):
        off = pad_g - (CF_CONV - 1) + k
        phases.setdefault(off % sub, []).append((k, off - off % sub))
    for c0 in range(0, width, conv_lanes):
        cols = slice(c0, c0 + conv_lanes)
        acc = None
        for p, taps in sorted(phases.items()):
            nrows = ts if p == 0 else ts + sub
            z = None
            for k, q in taps:
                term = cfw_ref[k:k + 1, cols] * ge_ref[q:q + nrows, cols]
                z = term if z is None else z + term
            part = z if p == 0 else z[p:p + ts]
            acc = part if acc is None else acc + part
        d_ref[:, cols] = acc + cfb_ref[:, cols]

    for r0 in range(0, ts, rows):
        base = pad_p - (SC_CONV - 1) + r0
        acc = scw_ref[0:1, :] * pe_ref[base:base + rows, :]
        for k in range(1, SC_CONV):
            acc = acc + scw_ref[k:k + 1, :] * pe_ref[base + k:base + k + rows, :]
        y_c = sb_ref[r0:r0 + rows, :] * acc

        d = d_ref[r0:r0 + rows, :]
        mu = jnp.mean(d, axis=-1, keepdims=True)
        dc = d - mu
        var = jnp.mean(dc * dc, axis=-1, keepdims=True)
        z = (dc * lax.rsqrt(var + EPS)) * lng_ref[...] + lnb_ref[...]
        store(r0, y_c, z * _sigmoid(z))

    pe_ref[0:pad_p, :] = pe_ref[ts:ts + pad_p, :]
    ge_ref[0:pad_g, :] = ge_ref[ts:ts + pad_g, :]


def _odd_body(sb_ref, sc_ref, sv_ref, cu_ref, cg_ref, scw_ref, cfw_ref, cfb_ref, lng_ref, lnb_ref,
              yc_ref, yd_ref, pe_ref, ge_ref, d_ref, *, ts, rows, conv_lanes):
    @pl.when(pl.program_id(1) == 0)
    def _():
        _odd_reset(pe_ref, ge_ref)

    def store(r0, y_c, y_d):
        yc_ref[r0:r0 + rows, :] = y_c.astype(yc_ref.dtype)
        yd_ref[r0:r0 + rows, :] = y_d.astype(yd_ref.dtype)

    _odd_tile(sb_ref, sc_ref, sv_ref, cu_ref, cg_ref, scw_ref, cfw_ref, cfb_ref, lng_ref, lnb_ref,
              pe_ref, ge_ref, d_ref, store, ts=ts, rows=rows, conv_lanes=conv_lanes)


def odd_mixer(proj_out, batch, sc_w, cf_w, cf_b, ln_g, ln_b, layer, ts=256, rows=32, conv_lanes=128):
    T = proj_out.shape[0]
    W = proj_out.shape[1] // 5
    nt = T // batch // ts
    assert ts % rows == 0 and W % conv_lanes == 0 and ts >= ODD_PAD_G

    def col(section):
        return pl.BlockSpec((ts, W), lambda b, i: (b * nt + i, section))

    vec = pl.BlockSpec((None, 1, W), lambda b, i: (layer, 0, 0))
    out = pl.BlockSpec((ts, W), lambda b, i: (b * nt + i, 0))
    return pl.pallas_call(
        functools.partial(_odd_body, ts=ts, rows=rows, conv_lanes=conv_lanes),
        grid=(batch, nt),
        in_specs=[
            col(0), col(1), col(2), col(3), col(4),
            pl.BlockSpec((None, SC_CONV, W), lambda b, i: (layer, 0, 0)),
            pl.BlockSpec((None, CF_CONV, W), lambda b, i: (layer, 0, 0)),
            vec, vec, vec,
        ],
        out_specs=[out, out],
        out_shape=[jax.ShapeDtypeStruct((T, W), BF16), jax.ShapeDtypeStruct((T, W), BF16)],
        scratch_shapes=[
            pltpu.VMEM((ts + 8, W), F32),
            pltpu.VMEM((ts + 32, W), F32),
            pltpu.VMEM((ts, W), F32),
        ],
        compiler_params=_params("arbitrary", "arbitrary"),
        name="odd_mixer",
    )(proj_out, proj_out, proj_out, proj_out, proj_out, sc_w, cf_w, cf_b, ln_g, ln_b)


def kernel(x, ln_mix_g, ln_ffn_g, ln_final_g, ev_w_in, ev_b_in, lru_conv_w, lru_conv_b, lru_wa, lru_ba, lru_wx, lru_bx, lru_lambda, hgrn_lb_logits, hgrn_norm_g, ev_w_out, od_w_in, od_b_in, sc_conv_w, cf_conv_w, cf_conv_b, cf_ln_g, cf_ln_b, od_w_out, ffn_w_gate, ffn_w_up, ffn_w_down):
    B, S, D = x.shape
    depth = ln_mix_g.shape[0]
    xt = x.reshape(B * S, D)

    def rows(p):
        return p.reshape(p.shape[0], 1, p.shape[-1])

    ln_mix = rows(ln_mix_g)
    ln_ffn = rows(ln_ffn_g)
    ln_fin = ln_final_g.reshape(1, 1, D)
    ev_b = rows(ev_b_in)
    od_b = rows(od_b_in)
    w_gates = jnp.concatenate([lru_wa, lru_wx], axis=-1)
    n_even = lru_wa.shape[0]
    lru_ba2 = lru_ba.reshape(n_even, 1, -1)
    lru_bx2 = lru_bx.reshape(n_even, 1, -1)

    xg, ss = prenorm(xt, ln_mix, 0)
    for layer in range(depth):
        j = layer // 2
        if layer % 2 == 0:
            p = proj(xg, ss, ev_w_in, ev_b, j)
            ya = lru_mixer(p, B, lru_conv_w, rows(lru_conv_b), w_gates, lru_ba2, lru_bx2,
                           rows(lru_lambda), j)
            yb = hgrn_mixer(p, B, hgrn_lb_logits, rows(hgrn_norm_g), j)
            xt, xg, ss = out_proj(ya, yb, ev_w_out, j, xt, ln_ffn, layer)
        else:
            p = proj(xg, ss, od_w_in, od_b, j)
            yc, yd = odd_mixer(p, B, sc_conv_w, cf_conv_w, rows(cf_conv_b), rows(cf_ln_g),
                               rows(cf_ln_b), j)
            xt, xg, ss = out_proj(yc, yd, od_w_out, j, xt, ln_ffn, layer)
        a = ffn_up(xg, ss, ffn_w_gate, ffn_w_up, layer)
        if layer + 1 < depth:
            xt, xg, ss = ffn_down(a, ffn_w_down, layer, xt, ln_mix, layer + 1)
        else:
            xt = ffn_down(a, ffn_w_down, layer, xt, None, 0)
    out = rmsnorm(xt, ln_fin, 0, F32)
    return out.reshape(B, S, D)
```

```python
import functools

import jax
import jax.numpy as jnp
from jax import lax
from jax.experimental import pallas as pl
from jax.experimental.pallas import tpu as pltpu

F32 = jnp.float32
BF16 = jnp.bfloat16

EPS = 1e-6
F_FLOOR = 1e-30
LRU_C = 8.0
LRU_HEADS = 8
LRU_CONV = 4
HGRN_HEADS = 8
HEAD_DIM = 128
SC_CONV = 3
CF_CONV = 31
HGRN_SUB = 32
LOG2_E = 1.4426950408889634

SS_LANES = 8
V7X_SUBLANES = 8
V7X_VMEM_BYTES = 64 * 2**20
MIB = 2**20
KERNEL_VMEM_LIMIT = V7X_VMEM_BYTES - 8 * MIB

TILE_IN = (1024, 1024)
TILE_OUT = (512, 2048)
TILE_UP = (1024, 512)
TILE_DOWN = (512, 512)


def _params(*semantics):
    return pltpu.CompilerParams(dimension_semantics=semantics, vmem_limit_bytes=KERNEL_VMEM_LIMIT)


def _sigmoid(x):
    return jax.nn.sigmoid(x)


def _sumsq_lanes(x):
    return jnp.broadcast_to(jnp.sum(x * x, axis=-1, keepdims=True), (x.shape[0], SS_LANES))


def _row_scale(ss_ref, d_model):
    ms = jnp.sum(ss_ref[...], axis=0)[:, 0:1] * (1.0 / d_model)
    return lax.rsqrt(ms + EPS)


def _prenorm_body(x_ref, g_ref, xg_ref, ss_ref):
    x = x_ref[...]
    xg_ref[...] = (x * g_ref[...]).astype(xg_ref.dtype)
    ss_ref[...] = _sumsq_lanes(x)


def prenorm(x, g_stack, layer, tm=512):
    T, D = x.shape
    return pl.pallas_call(
        _prenorm_body,
        grid=(T // tm,),
        in_specs=[
            pl.BlockSpec((tm, D), lambda i: (i, 0)),
            pl.BlockSpec((None, 1, D), lambda i: (layer, 0, 0)),
        ],
        out_specs=[
            pl.BlockSpec((tm, D), lambda i: (i, 0)),
            pl.BlockSpec((None, tm, SS_LANES), lambda i: (0, i, 0)),
        ],
        out_shape=[jax.ShapeDtypeStruct((T, D), BF16),
                   jax.ShapeDtypeStruct((1, T, SS_LANES), F32)],
        compiler_params=_params("arbitrary"),
        name="prenorm",
    )(x, g_stack)


def _rmsnorm_body(x_ref, g_ref, o_ref):
    x = x_ref[...]
    ms = jnp.mean(x * x, axis=-1, keepdims=True)
    o_ref[...] = ((x * lax.rsqrt(ms + EPS)) * g_ref[...]).astype(o_ref.dtype)


def rmsnorm(x, g_stack, layer, out_dtype, tm=512):
    T, D = x.shape
    return pl.pallas_call(
        _rmsnorm_body,
        grid=(T // tm,),
        in_specs=[
            pl.BlockSpec((tm, D), lambda i: (i, 0)),
            pl.BlockSpec((None, 1, D), lambda i: (layer, 0, 0)),
        ],
        out_specs=pl.BlockSpec((tm, D), lambda i: (i, 0)),
        out_shape=jax.ShapeDtypeStruct((T, D), out_dtype),
        compiler_params=_params("arbitrary"),
        name="rmsnorm",
    )(x, g_stack)


def _proj_body(xg_ref, ss_ref, w_ref, b_ref, o_ref, wbf_ref, *, d_model):
    @pl.when(pl.program_id(1) == 0)
    def _():
        wbf_ref[...] = w_ref[...].astype(BF16)

    acc = jnp.dot(xg_ref[...], wbf_ref[...], preferred_element_type=F32)
    o_ref[...] = _row_scale(ss_ref, d_model) * acc + b_ref[...]


def proj(xg, ss, w_stack, b_stack, layer):
    tm, tn = TILE_IN
    T, K = xg.shape
    N = w_stack.shape[-1]
    parts = ss.shape[0]
    return pl.pallas_call(
        functools.partial(_proj_body, d_model=K),
        grid=(N // tn, T // tm),
        in_specs=[
            pl.BlockSpec((tm, K), lambda n, m: (m, 0)),
            pl.BlockSpec((parts, tm, SS_LANES), lambda n, m: (0, m, 0)),
            pl.BlockSpec((None, K, tn), lambda n, m: (layer, 0, n)),
            pl.BlockSpec((None, 1, tn), lambda n, m: (layer, 0, n)),
        ],
        out_specs=pl.BlockSpec((tm, tn), lambda n, m: (m, n)),
        out_shape=jax.ShapeDtypeStruct((T, N), F32),
        scratch_shapes=[pltpu.VMEM((K, tn), BF16)],
        compiler_params=_params("arbitrary", "arbitrary"),
        name="in_proj",
    )(xg, ss, w_stack, b_stack)


def _emit_residual(xn, gn_ref, o_ref, xg_ref, ss_ref):
    o_ref[...] = xn
    if gn_ref is not None:
        xg_ref[...] = (xn * gn_ref[...]).astype(xg_ref.dtype)
        ss_ref[...] = _sumsq_lanes(xn)


def _out_body(*refs, emit_norm):
    if emit_norm:
        ya_ref, yb_ref, wa_ref, wb_ref, x_ref, gn_ref, o_ref, xg_ref, ss_ref, wa_bf, wb_bf = refs
    else:
        ya_ref, yb_ref, wa_ref, wb_ref, x_ref, o_ref, wa_bf, wb_bf = refs
        gn_ref = xg_ref = ss_ref = None

    @pl.when(pl.program_id(1) == 0)
    def _():
        wa_bf[...] = wa_ref[...].astype(BF16)
        wb_bf[...] = wb_ref[...].astype(BF16)

    acc = jnp.dot(ya_ref[...], wa_bf[...], preferred_element_type=F32)
    acc = acc + jnp.dot(yb_ref[...], wb_bf[...], preferred_element_type=F32)
    _emit_residual(x_ref[...] + acc, gn_ref, o_ref, xg_ref, ss_ref)


def _residual_specs(T, N, tm, tn, next_gain, next_layer):
    x_spec = pl.BlockSpec((tm, tn), lambda n, m: (m, n))
    if next_gain is None:
        return [], [], x_spec, jax.ShapeDtypeStruct((T, N), F32)
    ins = [pl.BlockSpec((None, 1, tn), lambda n, m: (next_layer, 0, n))]
    outs = [x_spec, x_spec, pl.BlockSpec((None, tm, SS_LANES), lambda n, m: (n, m, 0))]
    shapes = [jax.ShapeDtypeStruct((T, N), F32), jax.ShapeDtypeStruct((T, N), BF16),
              jax.ShapeDtypeStruct((N // tn, T, SS_LANES), F32)]
    return ins, [next_gain], outs, shapes


def out_proj(ya, yb, w_stack, layer, x, next_gain, next_layer):
    tm, tn = TILE_OUT
    T, Kh = ya.shape
    N = w_stack.shape[-1]
    g_in, g_arg, outs, shapes = _residual_specs(T, N, tm, tn, next_gain, next_layer)
    return pl.pallas_call(
        functools.partial(_out_body, emit_norm=next_gain is not None),
        grid=(N // tn, T // tm),
        in_specs=[
            pl.BlockSpec((tm, Kh), lambda n, m: (m, 0)),
            pl.BlockSpec((tm, Kh), lambda n, m: (m, 0)),
            pl.BlockSpec((None, Kh, tn), lambda n, m: (layer, 0, n), pipeline_mode=pl.Buffered(1)),
            pl.BlockSpec((None, Kh, tn), lambda n, m: (layer, 1, n), pipeline_mode=pl.Buffered(1)),
            pl.BlockSpec((tm, tn), lambda n, m: (m, n)),
        ] + g_in,
        out_specs=outs,
        out_shape=shapes,
        scratch_shapes=[pltpu.VMEM((Kh, tn), BF16), pltpu.VMEM((Kh, tn), BF16)],
        compiler_params=_params("arbitrary", "arbitrary"),
        name="out_proj",
    )(ya, yb, w_stack, w_stack, x, *g_arg)


def _ffn_up_body(xg_ref, ss_ref, wg_ref, wu_ref, o_ref, wg_bf, wu_bf, *, d_model):
    @pl.when(pl.program_id(1) == 0)
    def _():
        wg_bf[...] = wg_ref[...].astype(BF16)
        wu_bf[...] = wu_ref[...].astype(BF16)

    h = xg_ref[...]
    rs = _row_scale(ss_ref, d_model)
    g = rs * jnp.dot(h, wg_bf[...], preferred_element_type=F32)
    u = rs * jnp.dot(h, wu_bf[...], preferred_element_type=F32)
    o_ref[...] = ((g * _sigmoid(g)) * u).astype(o_ref.dtype)


def ffn_up(xg, ss, wg_stack, wu_stack, layer):
    tm, tn = TILE_UP
    T, K = xg.shape
    F = wg_stack.shape[-1]
    parts = ss.shape[0]
    return pl.pallas_call(
        functools.partial(_ffn_up_body, d_model=K),
        grid=(F // tn, T // tm),
        in_specs=[
            pl.BlockSpec((tm, K), lambda n, m: (m, 0)),
            pl.BlockSpec((parts, tm, SS_LANES), lambda n, m: (0, m, 0)),
            pl.BlockSpec((None, K, tn), lambda n, m: (layer, 0, n)),
            pl.BlockSpec((None, K, tn), lambda n, m: (layer, 0, n)),
        ],
        out_specs=pl.BlockSpec((tm, tn), lambda n, m: (m, n)),
        out_shape=jax.ShapeDtypeStruct((T, F), BF16),
        scratch_shapes=[pltpu.VMEM((K, tn), BF16), pltpu.VMEM((K, tn), BF16)],
        compiler_params=_params("arbitrary", "arbitrary"),
        name="ffn_up",
    )(xg, ss, wg_stack, wu_stack)


def _ffn_down_body(*refs, emit_norm):
    if emit_norm:
        a_ref, w_ref, x_ref, gn_ref, o_ref, xg_ref, ss_ref, wbf_ref = refs
    else:
        a_ref, w_ref, x_ref, o_ref, wbf_ref = refs
        gn_ref = xg_ref = ss_ref = None

    @pl.when(pl.program_id(1) == 0)
    def _():
        wbf_ref[...] = w_ref[...].astype(BF16)

    acc = jnp.dot(a_ref[...], wbf_ref[...], preferred_element_type=F32)
    _emit_residual(x_ref[...] + acc, gn_ref, o_ref, xg_ref, ss_ref)


def ffn_down(a, w_stack, layer, x, next_gain, next_layer):
    tm, tn = TILE_DOWN
    T, F = a.shape
    N = w_stack.shape[-1]
    g_in, g_arg, outs, shapes = _residual_specs(T, N, tm, tn, next_gain, next_layer)
    return pl.pallas_call(
        functools.partial(_ffn_down_body, emit_norm=next_gain is not None),
        grid=(N // tn, T // tm),
        in_specs=[
            pl.BlockSpec((tm, F), lambda n, m: (m, 0)),
            pl.BlockSpec((None, F, tn), lambda n, m: (layer, 0, n)),
            pl.BlockSpec((tm, tn), lambda n, m: (m, n)),
        ] + g_in,
        out_specs=outs,
        out_shape=shapes,
        scratch_shapes=[pltpu.VMEM((F, tn), BF16)],
        compiler_params=_params("arbitrary", "arbitrary"),
        name="ffn_down",
    )(a, w_stack, x, *g_arg)


def _lru_body(xa_ref, gate_ref, cw_ref, cb_ref, wg_ref, ba_ref, bx_ref, lam_ref, y_ref,
              xe_ref, hc_ref, *, ts):
    i = pl.program_id(1)
    pad = V7X_SUBLANES
    width = xa_ref.shape[1]

    @pl.when(i == 0)
    def _():
        xe_ref[0:pad, :] = jnp.zeros((pad, width), F32)
        hc_ref[...] = jnp.zeros(hc_ref.shape, F32)

    xe_ref[pad:pad + ts, :] = xa_ref[...]
    tiles = ts // pad
    row = lax.broadcasted_iota(jnp.int32, (pad, HEAD_DIM), 0)
    subrow = lax.broadcasted_iota(jnp.int32, (tiles, pad, HEAD_DIM), 1)

    for h in range(LRU_HEADS):
        cols = slice(h * HEAD_DIM, (h + 1) * HEAD_DIM)
        x3 = xe_ref[pad:pad + ts, cols].reshape(tiles, pad, HEAD_DIM)
        hist = xe_ref[0:pad, cols].reshape(1, pad, HEAD_DIM)
        xc3 = cw_ref[LRU_CONV - 1:LRU_CONV, cols] * x3
        for s in range(1, LRU_CONV):
            cur = pltpu.roll(x3, s, 1)
            prev = jnp.concatenate([pltpu.roll(hist, s, 1), cur[:tiles - 1]], axis=0)
            xc3 = xc3 + cw_ref[LRU_CONV - 1 - s:LRU_CONV - s, cols] * jnp.where(subrow >= s, cur, prev)
        xc = xc3.reshape(ts, HEAD_DIM) + cb_ref[:, cols]

        gh = jnp.dot(xc.astype(BF16), wg_ref[h].astype(BF16), preferred_element_type=F32)
        r = _sigmoid(gh[:, :HEAD_DIM] + ba_ref[:, cols])
        ig = _sigmoid(gh[:, HEAD_DIM:] + bx_ref[:, cols])

        nl = -lam_ref[:, cols]
        softplus = jnp.maximum(nl, 0.0) + jnp.log1p(jnp.exp(-jnp.abs(nl)))
        log_a = (-LRU_C * r) * softplus
        a = jnp.exp(log_a)
        mult = jnp.sqrt(jnp.maximum(jnp.tanh(-log_a) * (a * a + 1.0), 0.0))
        first = jnp.where(jnp.logical_and(i == 0, row == 0), 1.0, mult[:pad])
        mult = jnp.concatenate([first, mult[pad:]], axis=0)
        u = (mult * ig) * xc

        a3 = a.reshape(tiles, pad, HEAD_DIM)
        u3 = u.reshape(tiles, pad, HEAD_DIM)
        d = 1
        while d < pad:
            keep = subrow >= d
            a_s = jnp.where(keep, pltpu.roll(a3, d, 1), 1.0)
            u_s = jnp.where(keep, pltpu.roll(u3, d, 1), 0.0)
            u3 = a3 * u_s + u3
            a3 = a3 * a_s
            d *= 2
        h_prev = hc_ref[0:1, cols]
        h_tiles = []
        for t in range(tiles):
            h_tile = a3[t] * h_prev + u3[t]
            h_tiles.append(h_tile)
            h_prev = h_tile[pad - 1:pad, :]
        hc_ref[0:1, cols] = h_prev
        hseq = jnp.concatenate(h_tiles, axis=0)

        g = gate_ref[:, cols]
        c = 0.7978845608028654
        cdf = 0.5 * (1.0 + jnp.tanh(c * (g + 0.044715 * (g * g * g))))
        y_ref[:, cols] = (hseq * (g * cdf)).astype(y_ref.dtype)
    xe_ref[0:pad, :] = xe_ref[ts:ts + pad, :]


def lru_mixer(proj_out, batch, conv_w, conv_b, w_gates, ba, bx, lam, layer, ts=256):
    T = proj_out.shape[0]
    W = LRU_HEADS * HEAD_DIM
    nt = T // batch // ts
    vec = pl.BlockSpec((None, 1, W), lambda b, i: (layer, 0, 0))
    return pl.pallas_call(
        functools.partial(_lru_body, ts=ts),
        grid=(batch, nt),
        in_specs=[
            pl.BlockSpec((ts, W), lambda b, i: (b * nt + i, 0)),
            pl.BlockSpec((ts, W), lambda b, i: (b * nt + i, 1)),
            pl.BlockSpec((None, LRU_CONV, W), lambda b, i: (layer, 0, 0)),
            vec,
            pl.BlockSpec((None, LRU_HEADS, HEAD_DIM, 2 * HEAD_DIM), lambda b, i: (layer, 0, 0, 0)),
            vec, vec, vec,
        ],
        out_specs=pl.BlockSpec((ts, W), lambda b, i: (b * nt + i, 0)),
        out_shape=jax.ShapeDtypeStruct((T, W), BF16),
        scratch_shapes=[
            pltpu.VMEM((ts + 8, W), F32),
            pltpu.VMEM((8, W), F32),
        ],
        compiler_params=_params("arbitrary", "arbitrary"),
        name="rg_lru",
    )(proj_out, proj_out, conv_w, conv_b, w_gates, ba, bx, lam)


def _cumsum_rows(x):
    n, width = x.shape
    sub = V7X_SUBLANES
    x3 = x.reshape(n // sub, sub, width)
    subrow = lax.broadcasted_iota(jnp.int32, x3.shape, 1)
    d = 1
    while d < sub:
        x3 = x3 + jnp.where(subrow >= d, pltpu.roll(x3, d, 1), 0.0)
        d *= 2
    tiles = [x3[0]]
    for t in range(1, n // sub):
        tiles.append(x3[t] + tiles[-1][sub - 1:sub, :])
    return jnp.concatenate(tiles, axis=0)


def _hgrn_body(q_ref, f_ref, v_ref, g_ref, lbl_ref, ng_ref, y_ref, st_ref, rows_ref, *, ts, layer):
    @pl.when(pl.program_id(0) == 0)
    def _():
        st_ref[...] = jnp.zeros(st_ref.shape, F32)

    logits = lbl_ref[...]
    e = jnp.exp(logits - jnp.max(logits, axis=0, keepdims=True))
    sm = e / jnp.sum(e, axis=0, keepdims=True)
    lb = jnp.sum(sm[0:layer + 1, :], axis=0, keepdims=True) - sm[0:1, :]
    one_m_lb = 1.0 - lb
    ng = ng_ref[...]
    n = HGRN_SUB
    sub = V7X_SUBLANES
    batch, _, width = q_ref.shape
    subrow = lax.broadcasted_iota(jnp.int32, (sub, width), 0)
    nt_dims = (((1,), (1,)), ((), ()))
    tn_dims = (((0,), (0,)), ((), ()))

    t_idx = lax.broadcasted_iota(jnp.int32, (n, n), 0)
    s_idx = lax.broadcasted_iota(jnp.int32, (n, n), 1)
    diag_mask = t_idx == s_idx
    level_mask = {}
    h = 1
    while h < n:
        level_mask[h] = jnp.logical_and(t_idx // (2 * h) == s_idx // (2 * h),
                                        jnp.logical_and((t_idx // h) % 2 == 1, (s_idx // h) % 2 == 0))
        h *= 2

    def ref_rows(b, h):
        pieces = []
        for i in range(n // sub):
            base = i * sub
            if 2 * h >= sub:
                rho = (base // (2 * h)) * (2 * h) + h - 1
                pieces.append(jnp.broadcast_to(rows_ref[b, pl.ds(rho, 1), :], (sub, width)))
            else:
                piece = None
                for c in range(sub // (2 * h)):
                    rho = base + c * 2 * h + h - 1
                    rowv = jnp.broadcast_to(rows_ref[b, pl.ds(rho, 1), :], (sub, width))
                    piece = rowv if piece is None else jnp.where(subrow >= c * 2 * h, rowv, piece)
                pieces.append(piece)
        return jnp.concatenate(pieces, axis=0)

    def sub_block(b, j):
        r0 = pl.multiple_of(j * n, n)
        f = f_ref[b, pl.ds(r0, n), :]
        q = q_ref[b, pl.ds(r0, n), :]
        v = v_ref[b, pl.ds(r0, n), :]
        sig = _sigmoid(f)
        fg = jnp.maximum(lb + one_m_lb * sig, F_FLOOR)
        k = one_m_lb * (1.0 - sig)
        qf = q * _sigmoid(q)
        bc = _cumsum_rows(jnp.log(fg)) * LOG2_E
        rows_ref[b] = bc
        b_tot = bc[n - 1:n, :]
        qb = qf.astype(BF16)
        kb = k.astype(BF16)
        vb = v.astype(BF16)
        qd = qb * jnp.exp2(bc).astype(BF16)
        kd = kb * jnp.exp2(b_tot - bc).astype(BF16)
        e_tot = jnp.exp2(b_tot)
        q01 = jnp.concatenate([qb, qb * fg.astype(BF16)], axis=0)
        levels = []
        h = n // 2
        while h >= 2:
            br = ref_rows(b, h)
            dist = bc - br
            dec = jnp.exp2(jnp.minimum(dist, -dist)).astype(BF16)
            levels.append((h, qb * dec, kb * dec))
            h //= 2

        heads = [slice(hd * HEAD_DIM, (hd + 1) * HEAD_DIM) for hd in range(HGRN_HEADS)]
        ws, inters = [], []
        for hd, sl in enumerate(heads):
            r01 = lax.dot_general(q01[:, sl], kb[:, sl], nt_dims, preferred_element_type=F32)
            w = jnp.where(diag_mask, r01[:n], 0.0)
            w = jnp.where(level_mask[1], r01[n:], w)
            for h, ql, kl in levels:
                r = lax.dot_general(ql[:, sl], kl[:, sl], nt_dims, preferred_element_type=F32)
                w = jnp.where(level_mask[h], r, w)
            ws.append(w.astype(BF16))
            st = st_ref[b, hd]
            inters.append(lax.dot_general(qd[:, sl], st.astype(BF16), nt_dims,
                                          preferred_element_type=F32))
            d_st = lax.dot_general(vb[:, sl], kd[:, sl], tn_dims, preferred_element_type=F32)
            st_ref[b, hd] = st * e_tot[:, sl] + d_st
        outs = [inters[hd] + jnp.dot(ws[hd], vb[:, sl], preferred_element_type=F32)
                for hd, sl in enumerate(heads)]
        return jnp.concatenate(outs, axis=1)

    def finish(b, j, o_all):
        r0 = pl.multiple_of(j * n, n)
        g = g_ref[b, pl.ds(r0, n), :]
        outs = []
        for hd in range(HGRN_HEADS):
            o = o_all[:, hd * HEAD_DIM:(hd + 1) * HEAD_DIM]
            outs.append(o * lax.rsqrt(jnp.mean(o * o, axis=-1, keepdims=True) + EPS))
        o_n = jnp.concatenate(outs, axis=1) * ng
        y_ref[b, pl.ds(r0, n), :] = (o_n * (g * _sigmoid(g))).astype(y_ref.dtype)

    def step(j, o_prev):
        for b in range(batch):
            finish(b, j - 1, o_prev[b])
        return tuple(sub_block(b, j) for b in range(batch))

    n_sub = ts // n
    o_last = lax.fori_loop(1, n_sub, step, tuple(sub_block(b, 0) for b in range(batch)))
    for b in range(batch):
        finish(b, n_sub - 1, o_last[b])


def hgrn_mixer(proj_out, batch, lb_logits, norm_g, layer, ts=512):
    T = proj_out.shape[0]
    W = HGRN_HEADS * HEAD_DIM
    seq = T // batch
    n_layers = lb_logits.shape[0]
    p3 = proj_out.reshape(batch, seq, proj_out.shape[1])

    def col(section):
        return pl.BlockSpec((batch, ts, W), lambda i: (0, i, section))

    y = pl.pallas_call(
        functools.partial(_hgrn_body, ts=ts, layer=layer),
        grid=(seq // ts,),
        in_specs=[
            col(2), col(3), col(4), col(5),
            pl.BlockSpec((n_layers, W), lambda i: (0, 0)),
            pl.BlockSpec((None, 1, W), lambda i: (layer, 0, 0)),
        ],
        out_specs=pl.BlockSpec((batch, ts, W), lambda i: (0, i, 0)),
        out_shape=jax.ShapeDtypeStruct((batch, seq, W), BF16),
        scratch_shapes=[
            pltpu.VMEM((batch, HGRN_HEADS, HEAD_DIM, HEAD_DIM), F32),
            pltpu.VMEM((batch, HGRN_SUB, W), F32),
        ],
        compiler_params=_params("arbitrary"),
        name="hgrn2",
    )(p3, p3, p3, p3, lb_logits, norm_g)
    return y.reshape(T, W)


ODD_PAD_P = V7X_SUBLANES
ODD_PAD_G = 4 * V7X_SUBLANES


def _odd_reset(pe_ref, ge_ref):
    pe_ref[0:ODD_PAD_P, :] = jnp.zeros((ODD_PAD_P, pe_ref.shape[1]), F32)
    ge_ref[0:ODD_PAD_G, :] = jnp.zeros((ODD_PAD_G, ge_ref.shape[1]), F32)


def _odd_tile(sb_ref, sc_ref, sv_ref, cu_ref, cg_ref, scw_ref, cfw_ref, cfb_ref, lng_ref, lnb_ref,
              pe_ref, ge_ref, d_ref, store, *, ts, rows, conv_lanes):
    width = sb_ref.shape[1]
    sub = V7X_SUBLANES
    pad_p = ODD_PAD_P
    pad_g = ODD_PAD_G

    pe_ref[pad_p:pad_p + ts, :] = sc_ref[...] * sv_ref[...]
    ge_ref[pad_g:pad_g + ts, :] = cu_ref[...] * _sigmoid(cg_ref[...])

    phases = {}
    for k in range(CF_CONV):
        off = pad_g - (CF_CONV - 1) + k
        phases.setdefault(off % sub, []).append((k, off - off % sub))
    for c0 in range(0, width, conv_lanes):
        cols = slice(c0, c0 + conv_lanes)
        acc = None
        for p, taps in sorted(phases.items()):
            nrows = ts if p == 0 else ts + sub
            z = None
            for k, q in taps:
                term = cfw_ref[k:k + 1, cols] * ge_ref[q:q + nrows, cols]
                z = term if z is None else z + term
            part = z if p == 0 else z[p:p + ts]
            acc = part if acc is None else acc + part
        d_ref[:, cols] = acc + cfb_ref[:, cols]

    for r0 in range(0, ts, rows):
        base = pad_p - (SC_CONV - 1) + r0
        acc = scw_ref[0:1, :] * pe_ref[base:base + rows, :]
        for k in range(1, SC_CONV):
            acc = acc + scw_ref[k:k + 1, :] * pe_ref[base + k:base + k + rows, :]
        y_c = sb_ref[r0:r0 + rows, :] * acc

        d = d_ref[r0:r0 + rows, :]
        mu = jnp.mean(d, axis=-1, keepdims=True)
        dc = d - mu
        var = jnp.mean(dc * dc, axis=-1, keepdims=True)
        z = (dc * lax.rsqrt(var + EPS)) * lng_ref[...] + lnb_ref[...]
        store(r0, y_c, z * _sigmoid(z))

    pe_ref[0:pad_p, :] = pe_ref[ts:ts + pad_p, :]
    ge_ref[0:pad_g, :] = ge_ref[ts:ts + pad_g, :]


def _odd_body(sb_ref, sc_ref, sv_ref, cu_ref, cg_ref, scw_ref, cfw_ref, cfb_ref, lng_ref, lnb_ref,
              yc_ref, yd_ref, pe_ref, ge_ref, d_ref, *, ts, rows, conv_lanes):
    @pl.when(pl.program_id(1) == 0)
    def _():
        _odd_reset(pe_ref, ge_ref)

    def store(r0, y_c, y_d):
        yc_ref[r0:r0 + rows, :] = y_c.astype(yc_ref.dtype)
        yd_ref[r0:r0 + rows, :] = y_d.astype(yd_ref.dtype)

    _odd_tile(sb_ref, sc_ref, sv_ref, cu_ref, cg_ref, scw_ref, cfw_ref, cfb_ref, lng_ref, lnb_ref,
              pe_ref, ge_ref, d_ref, store, ts=ts, rows=rows, conv_lanes=conv_lanes)


def odd_mixer(proj_out, batch, sc_w, cf_w, cf_b, ln_g, ln_b, layer, ts=256, rows=32, conv_lanes=128):
    T = proj_out.shape[0]
    W = proj_out.shape[1] // 5
    nt = T // batch // ts
    assert ts % rows == 0 and W % conv_lanes == 0 and ts >= ODD_PAD_G

    def col(section):
        return pl.BlockSpec((ts, W), lambda b, i: (b * nt + i, section))

    vec = pl.BlockSpec((None, 1, W), lambda b, i: (layer, 0, 0))
    out = pl.BlockSpec((ts, W), lambda b, i: (b * nt + i, 0))
    return pl.pallas_call(
        functools.partial(_odd_body, ts=ts, rows=rows, conv_lanes=conv_lanes),
        grid=(batch, nt),
        in_specs=[
            col(0), col(1), col(2), col(3), col(4),
            pl.BlockSpec((None, SC_CONV, W), lambda b, i: (layer, 0, 0)),
            pl.BlockSpec((None, CF_CONV, W), lambda b, i: (layer, 0, 0)),
            vec, vec, vec,
        ],
        out_specs=[out, out],
        out_shape=[jax.ShapeDtypeStruct((T, W), BF16), jax.ShapeDtypeStruct((T, W), BF16)],
        scratch_shapes=[
            pltpu.VMEM((ts + 8, W), F32),
            pltpu.VMEM((ts + 32, W), F32),
            pltpu.VMEM((ts, W), F32),
        ],
        compiler_params=_params("arbitrary", "arbitrary"),
        name="odd_mixer",
    )(proj_out, proj_out, proj_out, proj_out, proj_out, sc_w, cf_w, cf_b, ln_g, ln_b)


def kernel(x, ln_mix_g, ln_ffn_g, ln_final_g, ev_w_in, ev_b_in, lru_conv_w, lru_conv_b, lru_wa, lru_ba, lru_wx, lru_bx, lru_lambda, hgrn_lb_logits, hgrn_norm_g, ev_w_out, od_w_in, od_b_in, sc_conv_w, cf_conv_w, cf_conv_b, cf_ln_g, cf_ln_b, od_w_out, ffn_w_gate, ffn_w_up, ffn_w_down):
    B, S, D = x.shape
    depth = ln_mix_g.shape[0]
    xt = x.reshape(B * S, D)

    def rows(p):
        return p.reshape(p.shape[0], 1, p.shape[-1])

    ln_mix = rows(ln_mix_g)
    ln_ffn = rows(ln_ffn_g)
    ln_fin = ln_final_g.reshape(1, 1, D)
    ev_b = rows(ev_b_in)
    od_b = rows(od_b_in)
    w_gates = jnp.concatenate([lru_wa, lru_wx], axis=-1)
    n_even = lru_wa.shape[0]
    lru_ba2 = lru_ba.reshape(n_even, 1, -1)
    lru_bx2 = lru_bx.reshape(n_even, 1, -1)

    xg, ss = prenorm(xt, ln_mix, 0)
    for layer in range(depth):
        j = layer // 2
        if layer % 2 == 0:
            p = proj(xg, ss, ev_w_in, ev_b, j)
            ya = lru_mixer(p, B, lru_conv_w, rows(lru_conv_b), w_gates, lru_ba2, lru_bx2,
                           rows(lru_lambda), j)
            yb = hgrn_mixer(p, B, hgrn_lb_logits, rows(hgrn_norm_g), j)
            xt, xg, ss = out_proj(ya, yb, ev_w_out, j, xt, ln_ffn, layer)
        else:
            p = proj(xg, ss, od_w_in, od_b, j)
            yc, yd = odd_mixer(p, B, sc_conv_w, cf_conv_w, rows(cf_conv_b), rows(cf_ln_g),
                               rows(cf_ln_b), j)
            xt, xg, ss = out_proj(yc, yd, od_w_out, j, xt, ln_ffn, layer)
        a = ffn_up(xg, ss, ffn_w_gate, ffn_w_up, layer)
        if layer + 1 < depth:
            xt, xg, ss = ffn_down(a, ffn_w_down, layer, xt, ln_mix, layer + 1)
        else:
            xt = ffn_down(a, ffn_w_down, layer, xt, None, 0)
    out = rmsnorm(xt, ln_fin, 0, F32)
    return out.reshape(B, S, D)
```
